```python
import math
import jax, jax.numpy as jnp
from jax import lax
import numpy as np

D_MODEL = 2048
BATCH = 8
SEQ = 2048
DEPTH = 4
DEC_BATCH = 4
DEC_SEQ = 4096
PAST_LEN = 128

HEAD_DIM = 128
N_HEADS_A = 8
N_HEADS_B = 8
DILATED_PAIRS = ((128, 1), (512, 4), (2048, 16))
GRID_W = 64
NA_KH = 8
NA_KW = 16
N_HEADS_C = 8
HEAD_DIM_C = D_MODEL // N_HEADS_C // 2
Q_BLOCK = 128
D_FF = 5632
N_EXPERTS = 8
TOP_K = 2
ROPE_THETA = 10000.0
EPS = 1e-6
N_EVEN = (DEPTH + 1) // 2
N_ODD = DEPTH // 2
W_EVEN_IN = 3 * (N_HEADS_A + N_HEADS_B) * HEAD_DIM
W_ODD_IN = 3 * 2 * N_HEADS_C * HEAD_DIM_C

kernel_name = "hybrid_dilated_na_diff_moe_encoder"


def rmsnorm(x, g):
    xf = x.astype(jnp.float32)
    y = xf * lax.rsqrt(jnp.mean(xf * xf, axis=-1, keepdims=True) + EPS)
    return (y * g.astype(jnp.float32)).astype(x.dtype)


def rope(x, pos):
    half = x.shape[-1] // 2
    inv = jnp.power(jnp.float32(ROPE_THETA), -jnp.arange(half, dtype=jnp.float32) / half)
    ang = pos.astype(jnp.float32)[:, None] * inv[None, :]
    cos = jnp.cos(ang)[:, None, :]
    sin = jnp.sin(ang)[:, None, :]
    xf = x.astype(jnp.float32)
    x1, x2 = xf[..., :half], xf[..., half:]
    return jnp.concatenate([x1 * cos - x2 * sin, x2 * cos + x1 * sin], axis=-1).astype(x.dtype)


def banded_attention(q, k, v, half_w):
    N, L, H, Dh = q.shape
    blk = half_w
    nb = -(-L // blk)
    Lp = nb * blk
    pad = Lp - L
    qp = jnp.pad(q, ((0, 0), (0, pad), (0, 0), (0, 0)))
    kp = jnp.pad(k, ((0, 0), (blk, pad + blk), (0, 0), (0, 0)))
    vp = jnp.pad(v, ((0, 0), (blk, pad + blk), (0, 0), (0, 0)))
    qb = qp.reshape(N, nb, blk, H, Dh)
    kb = kp.reshape(N, nb + 2, blk, H, Dh)
    vb = vp.reshape(N, nb + 2, blk, H, Dh)
    kw = jnp.concatenate([kb[:, :-2], kb[:, 1:-1], kb[:, 2:]], axis=2)
    vw = jnp.concatenate([vb[:, :-2], vb[:, 1:-1], vb[:, 2:]], axis=2)
    qpos = jnp.arange(Lp).reshape(nb, blk)
    kpos = jnp.arange(nb)[:, None] * blk - blk + jnp.arange(3 * blk)[None, :]
    mask = (jnp.abs(qpos[:, :, None] - kpos[:, None, :]) <= half_w) & (kpos[:, None, :] >= 0) & (kpos[:, None, :] < L)
    s = jnp.einsum('nbqhd,nbkhd->nbhqk', qb, kw).astype(jnp.float32) * (Dh ** -0.5)
    s = jnp.where(mask[None, :, None], s, -jnp.inf)
    lse = jax.nn.logsumexp(s, axis=-1)
    p = jnp.exp(s - lse[..., None])
    out = jnp.einsum('nbhqk,nbkhd->nbqhd', p.astype(v.dtype), vw).reshape(N, Lp, H, Dh)[:, :L]
    lse = jnp.transpose(lse, (0, 1, 3, 2)).reshape(N, Lp, H)[:, :L]
    return out, lse


def dilated_attention(q, k, v):
    B, S, H, Dh = q.shape
    outs, lses = [], []
    for (w, d) in DILATED_PAIRS:
        n_side = (w // 2) // d
        L = S // d
        def to_res(t):
            return t.reshape(B, L, d, H, Dh).transpose(0, 2, 1, 3, 4).reshape(B * d, L, H, Dh)
        o, l = banded_attention(to_res(q), to_res(k), to_res(v), n_side)
        outs.append(o.reshape(B, d, L, H, Dh).transpose(0, 2, 1, 3, 4).reshape(B, S, H, Dh))
        lses.append(l.reshape(B, d, L, H).transpose(0, 2, 1, 3).reshape(B, S, H))
    wts = jax.nn.softmax(jnp.stack(lses, axis=0), axis=0)
    out = jnp.sum(wts[..., None] * jnp.stack(outs, axis=0).astype(jnp.float32), axis=0)
    return out.astype(q.dtype)


def neighbourhood_attention(q, k, v, rpb):
    B, S, H, Dh = q.shape
    R = S // GRID_W
    kh = min(NA_KH, R)
    kw = NA_KW
    rows = jnp.arange(R)
    rs = jnp.clip(rows - kh // 2, 0, R - kh)
    row_idx = rs[:, None] + jnp.arange(kh)[None, :]
    cols = jnp.arange(GRID_W)
    cs = jnp.clip(cols - kw // 2, 0, GRID_W - kw)
    col_mask = (cols[None, :] >= cs[:, None]) & (cols[None, :] < cs[:, None] + kw)
    qg = q.reshape(B, R, GRID_W, H, Dh)
    k_rows = k.reshape(B, R, GRID_W, H, Dh)[:, row_idx]
    v_rows = v.reshape(B, R, GRID_W, H, Dh)[:, row_idx]
    s = jnp.einsum('brqhd,brjkhd->bhrqjk', qg, k_rows).astype(jnp.float32) * (Dh ** -0.5)
    roff = row_idx - rows[:, None] + (NA_KH - 1)
    coff = jnp.clip(cols[None, :] - cols[:, None], -(kw - 1), kw - 1) + (NA_KW - 1)
    bias = rpb.astype(jnp.float32)[:, roff[:, None, :, None], coff[None, :, None, :]]
    s = jnp.where(col_mask[:, None, :], s + bias[None], -jnp.inf)
    p = jax.nn.softmax(s.reshape(B, H, R, GRID_W, kh * GRID_W), axis=-1).reshape(s.shape)
    out = jnp.einsum('bhrqjk,brjkhd->brqhd', p.astype(v.dtype), v_rows)
    return out.reshape(B, S, H, Dh)


def even_mixer(h, w_in, w_out, rpb, pos):
    B, S, _ = h.shape
    wa = N_HEADS_A * HEAD_DIM
    wb = N_HEADS_B * HEAD_DIM
    qa, ka, va, qb, kb, vb = jnp.split(h @ w_in, [wa, 2 * wa, 3 * wa, 3 * wa + wb, 3 * wa + 2 * wb], axis=-1)
    heads = lambda t, n: t.reshape(B, S, n, HEAD_DIM)
    oa = dilated_attention(rope(heads(qa, N_HEADS_A), pos), rope(heads(ka, N_HEADS_A), pos), heads(va, N_HEADS_A))
    ob = neighbourhood_attention(heads(qb, N_HEADS_B), heads(kb, N_HEADS_B), heads(vb, N_HEADS_B), rpb)
    o = jnp.concatenate([oa, ob], axis=2).reshape(B, S, -1)
    return o @ w_out


def diff_attention(q, k, v, lam):
    B, S, H2, Dc = q.shape
    H = H2 // 2
    nb = S // Q_BLOCK
    qb = q.reshape(B, nb, Q_BLOCK, H2, Dc).transpose(1, 0, 2, 3, 4)
    def block(qi):
        s = jnp.einsum('bqhd,bkhd->bhqk', qi, k).astype(jnp.float32) * (Dc ** -0.5)
        p = jax.nn.softmax(s, axis=-1).reshape(B, H, 2, Q_BLOCK, S)
        a = p[:, :, 0] - lam * p[:, :, 1]
        return jnp.einsum('bhqk,bkhd->bqhd', a.astype(v.dtype), v)
    out = lax.map(block, qb)
    return out.transpose(1, 0, 2, 3, 4).reshape(B, S, H, 2 * Dc)


def diff_mixer(h, w_qkv, w_out, lq1, lk1, lq2, lk2, subln, pos, layer_idx):
    B, S, _ = h.shape
    wq = 2 * N_HEADS_C * HEAD_DIM_C
    q, k, v = jnp.split(h @ w_qkv, [wq, 2 * wq], axis=-1)
    q = rope(q.reshape(B, S, 2 * N_HEADS_C, HEAD_DIM_C), pos)
    k = rope(k.reshape(B, S, 2 * N_HEADS_C, HEAD_DIM_C), pos)
    v = v.reshape(B, S, N_HEADS_C, 2 * HEAD_DIM_C)
    lam_init = 0.8 - 0.6 * math.exp(-0.3 * layer_idx)
    lam = (jnp.exp(jnp.sum(lq1.astype(jnp.float32) * lk1.astype(jnp.float32)))
           - jnp.exp(jnp.sum(lq2.astype(jnp.float32) * lk2.astype(jnp.float32))) + lam_init)
    o = diff_attention(q, k, v, lam)
    o = rmsnorm(o, subln) * (1.0 - lam_init)
    return o.reshape(B, S, -1) @ w_out


def swiglu(h, wg, wu, wd):
    return (jax.nn.silu(h @ wg) * (h @ wu)) @ wd


def moe(h, router, wg, wu, wd):
    B, S, D = h.shape
    t = h.reshape(B * S, D)
    logits = (t @ router).astype(jnp.float32)
    vals, idx = lax.top_k(logits, TOP_K)
    gates = jax.nn.softmax(vals, axis=-1)
    G = jnp.sum(jax.nn.one_hot(idx, N_EXPERTS, dtype=jnp.float32) * gates[..., None], axis=1)
    y = jnp.zeros((B * S, D), jnp.float32)
    for e in range(N_EXPERTS):
        y = y + G[:, e:e + 1] * swiglu(t, wg[e], wu[e], wd[e]).astype(jnp.float32)
    return y.astype(h.dtype).reshape(B, S, D)


def trunk(x, ev_norm_mix, ev_w_in, ev_w_out, ev_rpb, ev_norm_ffn, ev_w_gate, ev_w_up, ev_w_down,
          od_norm_mix, od_w_qkv, od_w_out, od_lambda_q1, od_lambda_k1, od_lambda_q2, od_lambda_k2,
          od_subln, od_norm_ffn, od_router, od_w_gate, od_w_up, od_w_down, final_norm):
    S = x.shape[1]
    pos = jnp.arange(S)
    for i in range(DEPTH):
        j = i // 2
        if i % 2 == 0:
            x = x + even_mixer(rmsnorm(x, ev_norm_mix[j]), ev_w_in[j], ev_w_out[j], ev_rpb[j], pos)
            x = x + swiglu(rmsnorm(x, ev_norm_ffn[j]), ev_w_gate[j], ev_w_up[j], ev_w_down[j])
        else:
            x = x + diff_mixer(rmsnorm(x, od_norm_mix[j]), od_w_qkv[j], od_w_out[j], od_lambda_q1[j],
                               od_lambda_k1[j], od_lambda_q2[j], od_lambda_k2[j], od_subln[j], pos, i)
            x = x + moe(rmsnorm(x, od_norm_ffn[j]), od_router[j], od_w_gate[j], od_w_up[j], od_w_down[j])
    return rmsnorm(x, final_norm)


def setup_inputs(seed: int = 0) -> dict:
    key = jax.random.key(seed)
    ks = jax.random.split(key, 24)
    nrm = lambda k, shape, sc: jax.random.normal(k, shape, jnp.float32) * sc
    gain = lambda k, shape: 1.0 + 0.05 * jax.random.normal(k, shape, jnp.float32)
    D = D_MODEL
    return {
        "x_prompt": nrm(ks[0], (BATCH, SEQ, D), 1.0),
        "x_sample": nrm(ks[1], (DEC_BATCH, DEC_SEQ, D), 1.0),
        "ev_norm_mix": gain(ks[2], (N_EVEN, D)),
        "ev_w_in": nrm(ks[3], (N_EVEN, D, W_EVEN_IN), D ** -0.5),
        "ev_w_out": nrm(ks[4], (N_EVEN, (N_HEADS_A + N_HEADS_B) * HEAD_DIM, D), ((N_HEADS_A + N_HEADS_B) * HEAD_DIM) ** -0.5),
        "ev_rpb": nrm(ks[5], (N_EVEN, N_HEADS_B, 2 * NA_KH - 1, 2 * NA_KW - 1), 0.1),
        "ev_norm_ffn": gain(ks[6], (N_EVEN, D)),
        "ev_w_gate": nrm(ks[7], (N_EVEN, D, D_FF), D ** -0.5),
        "ev_w_up": nrm(ks[8], (N_EVEN, D, D_FF), D ** -0.5),
        "ev_w_down": nrm(ks[9], (N_EVEN, D_FF, D), D_FF ** -0.5),
        "od_norm_mix": gain(ks[10], (N_ODD, D)),
        "od_w_qkv": nrm(ks[11], (N_ODD, D, W_ODD_IN), D ** -0.5),
        "od_w_out": nrm(ks[12], (N_ODD, 2 * N_HEADS_C * HEAD_DIM_C, D), (2 * N_HEADS_C * HEAD_DIM_C) ** -0.5),
        "od_lambda_q1": nrm(ks[13], (N_ODD, HEAD_DIM_C), 0.1),
        "od_lambda_k1": nrm(ks[14], (N_ODD, HEAD_DIM_C), 0.1),
        "od_lambda_q2": nrm(ks[15], (N_ODD, HEAD_DIM_C), 0.1),
        "od_lambda_k2": nrm(ks[16], (N_ODD, HEAD_DIM_C), 0.1),
        "od_subln": gain(ks[17], (N_ODD, 2 * HEAD_DIM_C)),
        "od_norm_ffn": gain(ks[18], (N_ODD, D)),
        "od_router": nrm(ks[19], (N_ODD, D, N_EXPERTS), D ** -0.5),
        "od_w_gate": nrm(ks[20], (N_ODD, N_EXPERTS, D, D_FF), D ** -0.5),
        "od_w_up": nrm(ks[21], (N_ODD, N_EXPERTS, D, D_FF), D ** -0.5),
        "od_w_down": nrm(ks[22], (N_ODD, N_EXPERTS, D_FF, D), D_FF ** -0.5),
        "final_norm": gain(ks[23], (D,)),
    }


def reference(x_prompt, x_sample, ev_norm_mix, ev_w_in, ev_w_out, ev_rpb, ev_norm_ffn, ev_w_gate, ev_w_up,
              ev_w_down, od_norm_mix, od_w_qkv, od_w_out, od_lambda_q1, od_lambda_k1, od_lambda_q2,
              od_lambda_k2, od_subln, od_norm_ffn, od_router, od_w_gate, od_w_up, od_w_down, final_norm):
    y_prompt = trunk(x_prompt, ev_norm_mix, ev_w_in, ev_w_out, ev_rpb, ev_norm_ffn, ev_w_gate, ev_w_up, ev_w_down,
                     od_norm_mix, od_w_qkv, od_w_out, od_lambda_q1, od_lambda_k1, od_lambda_q2, od_lambda_k2,
                     od_subln, od_norm_ffn, od_router, od_w_gate, od_w_up, od_w_down, final_norm)
    y_sample = trunk(x_sample, ev_norm_mix, ev_w_in, ev_w_out, ev_rpb, ev_norm_ffn, ev_w_gate, ev_w_up, ev_w_down,
                     od_norm_mix, od_w_qkv, od_w_out, od_lambda_q1, od_lambda_k1, od_lambda_q2, od_lambda_k2,
                     od_subln, od_norm_ffn, od_router, od_w_gate, od_w_up, od_w_down, final_norm)
    return (y_prompt, y_sample)
```

```python
import functools
import math

import jax
import jax.numpy as jnp
from jax import lax
from jax.experimental import pallas as pl
from jax.experimental.pallas import tpu as pltpu

D_MODEL = 2048
DEPTH = 4
HEAD_DIM = 128
N_HEADS_A = 8
N_HEADS_B = 8
DILATED_PAIRS = ((128, 1), (512, 4), (2048, 16))
BAND_HALF = 64
GRID_W = 64
NA_KH = 8
NA_KW = 16
N_HEADS_C = 8
HEAD_DIM_C = 128
D_FF = 5632
N_EXPERTS = 8
ROPE_THETA = 10000.0
EPS = 1e-6
W_IN = 3 * (N_HEADS_A + N_HEADS_B) * HEAD_DIM
SCALE = HEAD_DIM ** -0.5

LANES = 128
VMEM_LIMIT_BYTES = 56 * 1024 * 1024
NEG_BIG = -1e30

BF16 = jnp.bfloat16
F32 = jnp.float32


def _params(*sem):
    return pltpu.CompilerParams(dimension_semantics=sem, vmem_limit_bytes=VMEM_LIMIT_BYTES)


def _rms_scale(x):
    return lax.rsqrt(jnp.mean(x * x, axis=-1, keepdims=True) + EPS)


def _norm_proj_kernel(x_ref, g_ref, w_ref, cos_ref, sin_ref, o_ref, h_ref, *, modes, tn):
    j = pl.program_id(1)

    @pl.when(j == 0)
    def _():
        x = x_ref[...]
        h_ref[...] = (x * _rms_scale(x) * g_ref[...]).astype(BF16)

    acc = jnp.dot(h_ref[...], w_ref[...], preferred_element_type=F32)

    def emit(rot, scale):
        for c in range(tn // LANES):
            blk = acc[:, c * LANES:(c + 1) * LANES]
            if rot:
                blk = blk * cos_ref[...] + pltpu.roll(blk, LANES // 2, 1) * sin_ref[...]
            if scale:
                blk = blk * SCALE
            o_ref[:, c * LANES:(c + 1) * LANES] = blk.astype(o_ref.dtype)

    for mode in sorted(set(modes)):
        tiles = [t for t, m in enumerate(modes) if m == mode]
        cond = functools.reduce(jnp.logical_or, [j == t for t in tiles])
        pl.when(cond)(functools.partial(emit, mode in (1, 2), mode in (2, 3)))


def norm_proj(x2d, gain, w, cos, sin, modes, seq, *, tm=512, tn=1024):
    n, d = x2d.shape
    wout = w.shape[1]
    assert n % tm == 0 and wout % tn == 0 and seq % tm == 0 and len(modes) == wout // tn
    ns = seq // tm
    return pl.pallas_call(
        functools.partial(_norm_proj_kernel, modes=tuple(modes), tn=tn),
        grid=(n // tm, wout // tn),
        in_specs=[
            pl.BlockSpec((tm, d), lambda i, j: (i, 0)),
            pl.BlockSpec((1, d), lambda i, j: (0, 0)),
            pl.BlockSpec((d, tn), lambda i, j: (0, j)),
            pl.BlockSpec((tm, LANES), lambda i, j: (i % ns, 0)),
            pl.BlockSpec((tm, LANES), lambda i, j: (i % ns, 0)),
        ],
        out_specs=pl.BlockSpec((tm, tn), lambda i, j: (i, j)),
        out_shape=jax.ShapeDtypeStruct((n, wout), BF16),
        scratch_shapes=[pltpu.VMEM((tm, d), BF16)],
        compiler_params=_params("parallel", "arbitrary"),
        name="norm_proj",
    )(x2d, gain.reshape(1, d), w, cos, sin)


def _band_kernel(*refs, length, first, last):
    if first:
        q_ref, k_ref, v_ref, o_ref, lse_ref = refs
    elif last:
        q_ref, k_ref, v_ref, op_ref, lp_ref, o_ref = refs
    else:
        q_ref, k_ref, v_ref, op_ref, lp_ref, o_ref, lse_ref = refs
    tq = LANES
    win = min(2 * tq, length)

    def body(t, carry):
        u0 = pl.multiple_of(t * tq, tq)
        ws = pl.multiple_of(jnp.clip(u0 - BAND_HALF, 0, length - win), BAND_HALF)
        q = q_ref[pl.ds(u0, tq), :]
        k = k_ref[pl.ds(ws, win), :]
        v = v_ref[pl.ds(ws, win), :]
        s = lax.dot_general(q, k, (((1,), (1,)), ((), ())), preferred_element_type=F32)
        qi = u0 + lax.broadcasted_iota(jnp.int32, (tq, win), 0)
        kj = ws + lax.broadcasted_iota(jnp.int32, (tq, win), 1)
        s = jnp.where(jnp.abs(qi - kj) <= BAND_HALF, s, NEG_BIG)
        m = jnp.max(s, axis=1, keepdims=True)
        p = jnp.exp(s - m)
        l = jnp.sum(p, axis=1, keepdims=True)
        o = jnp.dot(p.astype(BF16), v, preferred_element_type=F32) / l
        lse = jnp.broadcast_to(m + jnp.log(l), (tq, LANES))
        if not first:
            lse_p = lp_ref[pl.ds(u0, tq), :]
            o_p = op_ref[pl.ds(u0, tq), :]
            mx = jnp.maximum(lse_p, lse)
            e_p = jnp.exp(lse_p - mx)
            e_c = jnp.exp(lse - mx)
            tot = e_p + e_c
            o = (o_p * e_p + o * e_c) / tot
            lse = mx + jnp.log(tot)
        o_ref[pl.ds(u0, tq), :] = o.astype(o_ref.dtype)
        if not last:
            lse_ref[pl.ds(u0, tq), :] = lse
        return carry

    lax.fori_loop(0, length // tq, body, 0)


def dilated_attention(qkv, batch, seq):
    ncol = W_IN // LANES
    wa = N_HEADS_A * HEAD_DIM
    state = None
    for gi, (w, d) in enumerate(DILATED_PAIRS):
        assert (w // 2) // d == BAND_HALF and seq % (d * LANES) == 0
        first, last = gi == 0, gi == len(DILATED_PAIRS) - 1
        length = seq // d
        qkv_v = qkv.reshape(batch, length, d * W_IN)
        col = lambda off: pl.BlockSpec((None, length, LANES), lambda b, r, h, off=off: (b, 0, r * ncol + off + h))
        st_spec = pl.BlockSpec((None, length, LANES), lambda b, r, h: (b, 0, r * N_HEADS_A + h))
        in_specs = [col(0), col(N_HEADS_A), col(2 * N_HEADS_A)]
        args = [qkv_v, qkv_v, qkv_v]
        if not first:
            in_specs += [st_spec, st_spec]
            args += [state[0].reshape(batch, length, d * wa), state[1].reshape(batch, length, d * wa)]
        o_shape = jax.ShapeDtypeStruct((batch, length, d * wa), BF16 if last else F32)
        lse_shape = jax.ShapeDtypeStruct((batch, length, d * wa), F32)
        out = pl.pallas_call(
            functools.partial(_band_kernel, length=length, first=first, last=last),
            grid=(batch, d, N_HEADS_A),
            in_specs=in_specs,
            out_specs=st_spec if last else [st_spec, st_spec],
            out_shape=o_shape if last else [o_shape, lse_shape],
            compiler_params=_params("parallel", "parallel", "parallel"),
            name=f"band_attn_d{d}",
        )(*args)
        state = out
    return state.reshape(batch, seq, wa)


def na_bias_table(rpb):
    cols = jnp.arange(GRID_W)
    cs = jnp.clip(cols - NA_KW // 2, 0, GRID_W - NA_KW)
    col_mask = (cols[None, :] >= cs[:, None]) & (cols[None, :] < cs[:, None] + NA_KW)
    coff = jnp.clip(cols[None, :] - cols[:, None], -(NA_KW - 1), NA_KW - 1) + (NA_KW - 1)
    roff = jnp.arange(NA_KH)[None, :] - jnp.arange(NA_KH)[:, None] + (NA_KH - 1)
    bias = rpb.astype(F32)[:, roff[:, None, :, None], coff[None, :, None, :]]
    bias = jnp.where(col_mask[None, None, :, None, :], bias, NEG_BIG)
    return bias.reshape(rpb.shape[0], NA_KH, GRID_W, NA_KH * GRID_W)


def _na_kernel(q_ref, k_ref, v_ref, b_ref, o_ref, *, rows):
    nk = NA_KH * GRID_W

    def body(r, carry):
        rs = jnp.clip(r - NA_KH // 2, 0, rows - NA_KH)
        q = q_ref[pl.ds(pl.multiple_of(r * GRID_W, GRID_W), GRID_W), :]
        k0 = pl.multiple_of(rs * GRID_W, GRID_W)
        k = k_ref[pl.ds(k0, nk), :]
        v = v_ref[pl.ds(k0, nk), :]
        s = lax.dot_general(q, k, (((1,), (1,)), ((), ())), preferred_element_type=F32)
        s = s + b_ref[r - rs]
        m = jnp.max(s, axis=1, keepdims=True)
        p = jnp.exp(s - m)
        l = jnp.sum(p, axis=1, keepdims=True)
        o = jnp.dot(p.astype(BF16), v, preferred_element_type=F32) / l
        o_ref[pl.ds(pl.multiple_of(r * GRID_W, GRID_W), GRID_W), :] = o.astype(o_ref.dtype)
        return carry

    lax.fori_loop(0, rows, body, 0)


def neighbourhood_attention(qkv, bias, batch, seq):
    rows = seq // GRID_W
    assert rows >= NA_KH and seq % GRID_W == 0
    base = 3 * N_HEADS_A
    col = lambda off: pl.BlockSpec((None, seq, LANES), lambda b, h, off=off: (b, 0, base + off + h))
    return pl.pallas_call(
        functools.partial(_na_kernel, rows=rows),
        grid=(batch, N_HEADS_B),
        in_specs=[col(0), col(N_HEADS_B), col(2 * N_HEADS_B),
                  pl.BlockSpec((None, NA_KH, GRID_W, NA_KH * GRID_W), lambda b, h: (h, 0, 0, 0))],
        out_specs=pl.BlockSpec((None, seq, LANES), lambda b, h: (b, 0, h)),
        out_shape=jax.ShapeDtypeStruct((batch, seq, N_HEADS_B * HEAD_DIM), BF16),
        compiler_params=_params("parallel", "parallel"),
        name="na_attn",
    )(qkv, qkv, qkv, bias)


def _diff_kernel(q_ref, k_ref, v_ref, lq1_ref, lk1_ref, lq2_ref, lk2_ref, g_ref, o_ref, *, lam_init):
    lam = (jnp.exp(jnp.sum(lq1_ref[...] * lk1_ref[...], axis=1, keepdims=True))
           - jnp.exp(jnp.sum(lq2_ref[...] * lk2_ref[...], axis=1, keepdims=True)) + lam_init)
    v = v_ref[...]

    def half(c):
        q = q_ref[:, c * LANES:(c + 1) * LANES]
        k = k_ref[:, c * LANES:(c + 1) * LANES]
        s = lax.dot_general(q, k, (((1,), (1,)), ((), ())), preferred_element_type=F32)
        m = jnp.max(s, axis=1, keepdims=True)
        p = jnp.exp(s - m)
        l = jnp.sum(p, axis=1, keepdims=True)
        return jnp.dot(p.astype(BF16), v, preferred_element_type=F32) / l

    o = half(0) - lam * half(1)
    o = o * _rms_scale(o) * g_ref[...] * (1.0 - lam_init)
    o_ref[...] = o.astype(o_ref.dtype)


def diff_attention(qkv, lq1, lk1, lq2, lk2, subln, layer_idx, batch, seq, *, tq=256):
    hw = 2 * HEAD_DIM_C
    lam_init = 0.8 - 0.6 * math.exp(-0.3 * layer_idx)
    vec = lambda n: pl.BlockSpec((1, n), lambda b, h, t: (0, 0))
    return pl.pallas_call(
        functools.partial(_diff_kernel, lam_init=lam_init),
        grid=(batch, N_HEADS_C, seq // tq),
        in_specs=[pl.BlockSpec((None, tq, hw), lambda b, h, t: (b, t, h)),
                  pl.BlockSpec((None, seq, hw), lambda b, h, t: (b, 0, N_HEADS_C + h)),
                  pl.BlockSpec((None, seq, hw), lambda b, h, t: (b, 0, 2 * N_HEADS_C + h)),
                  vec(HEAD_DIM_C), vec(HEAD_DIM_C), vec(HEAD_DIM_C), vec(HEAD_DIM_C), vec(hw)],
        out_specs=pl.BlockSpec((None, tq, hw), lambda b, h, t: (b, t, h)),
        out_shape=jax.ShapeDtypeStruct((batch, seq, N_HEADS_C * hw), BF16),
        compiler_params=_params("parallel", "parallel", "arbitrary"),
        name="diff_attn",
    )(qkv, qkv, qkv, lq1.reshape(1, -1), lk1.reshape(1, -1), lq2.reshape(1, -1), lk2.reshape(1, -1),
      subln.reshape(1, -1))


def _proj_res_kernel(*refs, n_in):
    a_refs, w_ref, x_ref, o_ref = refs[:n_in], refs[n_in], refs[n_in + 1], refs[n_in + 2]
    acc = x_ref[...]
    k0 = 0
    for a_ref in a_refs:
        kk = a_ref.shape[1]
        acc = acc + jnp.dot(a_ref[...], w_ref[k0:k0 + kk, :], preferred_element_type=F32)
        k0 += kk
    o_ref[...] = acc


def proj_residual(acts, w, x2d, *, tm=512, tn=1024):
    n, d = x2d.shape
    ktot = w.shape[0]
    assert sum(a.shape[1] for a in acts) == ktot and n % tm == 0 and d % tn == 0
    return pl.pallas_call(
        functools.partial(_proj_res_kernel, n_in=len(acts)),
        grid=(n // tm, d // tn),
        in_specs=[pl.BlockSpec((tm, a.shape[1]), lambda i, j: (i, 0)) for a in acts]
        + [pl.BlockSpec((ktot, tn), lambda i, j: (0, j)), pl.BlockSpec((tm, tn), lambda i, j: (i, j))],
        out_specs=pl.BlockSpec((tm, tn), lambda i, j: (i, j)),
        out_shape=jax.ShapeDtypeStruct((n, d), F32),
        compiler_params=_params("parallel", "parallel"),
        name="proj_residual",
    )(*acts, w, x2d)


def _swiglu_kernel(*refs, gated):
    if gated:
        x_ref, g_ref, gate_ref, wg_ref, wu_ref, wd_ref, o_ref, h_ref, acc_ref = refs
    else:
        x_ref, g_ref, wg_ref, wu_ref, wd_ref, o_ref, h_ref, acc_ref = refs
    e, f = pl.program_id(1), pl.program_id(2)

    @pl.when((e == 0) & (f == 0))
    def _():
        x = x_ref[...]
        h_ref[...] = (x * _rms_scale(x) * g_ref[...]).astype(BF16)
        acc_ref[...] = x

    h = h_ref[...]
    gt = jnp.dot(h, wg_ref[...], preferred_element_type=F32)
    up = jnp.dot(h, wu_ref[...], preferred_element_type=F32)
    a = gt * jax.nn.sigmoid(gt) * up
    y = jnp.dot(a.astype(BF16), wd_ref[...], preferred_element_type=F32)
    if gated:
        lane = lax.broadcasted_iota(jnp.int32, gate_ref.shape, 1)
        y = y * jnp.sum(jnp.where(lane == e, gate_ref[...], 0.0), axis=1, keepdims=True)
    acc_ref[...] += y

    @pl.when((e == pl.num_programs(1) - 1) & (f == pl.num_programs(2) - 1))
    def _():
        o_ref[...] = acc_ref[...]


def swiglu_residual(x2d, gain, wg, wu, wd, gates=None, *, tm=512, tf=512):
    n, d = x2d.shape
    ne, _, ff = wg.shape
    assert n % tm == 0 and ff % tf == 0
    gated = gates is not None
    in_specs = [pl.BlockSpec((tm, d), lambda i, e, f: (i, 0)), pl.BlockSpec((1, d), lambda i, e, f: (0, 0))]
    args = [x2d, gain.reshape(1, d)]
    if gated:
        in_specs.append(pl.BlockSpec((tm, LANES), lambda i, e, f: (i, 0)))
        args.append(gates)
    in_specs += [pl.BlockSpec((None, d, tf), lambda i, e, f: (e, 0, f)),
                 pl.BlockSpec((None, d, tf), lambda i, e, f: (e, 0, f)),
                 pl.BlockSpec((None, tf, d), lambda i, e, f: (e, f, 0))]
    args += [wg, wu, wd]
    return pl.pallas_call(
        functools.partial(_swiglu_kernel, gated=gated),
        grid=(n // tm, ne, ff // tf),
        in_specs=in_specs,
        out_specs=pl.BlockSpec((tm, d), lambda i, e, f: (i, 0)),
        out_shape=jax.ShapeDtypeStruct((n, d), F32),
        scratch_shapes=[pltpu.VMEM((tm, d), BF16), pltpu.VMEM((tm, d), F32)],
        compiler_params=_params("parallel", "arbitrary", "arbitrary"),
        name="swiglu_moe" if gated else "swiglu",
    )(*args)


def _router_kernel(x_ref, g_ref, r_ref, o_ref):
    x = x_ref[...]
    h = x * _rms_scale(x) * g_ref[...]
    logits = jnp.dot(h, r_ref[...], preferred_element_type=F32, precision=lax.Precision.HIGHEST)
    lane = lax.broadcasted_iota(jnp.int32, logits.shape, 1)
    logits = jnp.where(lane < N_EXPERTS, logits, -jnp.inf)
    m1 = jnp.max(logits, axis=1, keepdims=True)
    i1 = jnp.min(jnp.where(logits == m1, lane, LANES), axis=1, keepdims=True)
    rest = jnp.where(lane == i1, -jnp.inf, logits)
    m2 = jnp.max(rest, axis=1, keepdims=True)
    i2 = jnp.min(jnp.where(rest == m2, lane, LANES), axis=1, keepdims=True)
    e2 = jnp.exp(m2 - m1)
    g1 = 1.0 / (1.0 + e2)
    g2 = e2 / (1.0 + e2)
    o_ref[...] = jnp.where(lane == i1, g1, 0.0) + jnp.where(lane == i2, g2, 0.0)


def router_gates(x2d, gain, router, *, tm=512):
    n, d = x2d.shape
    r_pad = jnp.zeros((d, LANES), F32).at[:, :N_EXPERTS].set(router.astype(F32))
    return pl.pallas_call(
        _router_kernel,
        grid=(n // tm,),
        in_specs=[pl.BlockSpec((tm, d), lambda i: (i, 0)), pl.BlockSpec((1, d), lambda i: (0, 0)),
                  pl.BlockSpec((d, LANES), lambda i: (0, 0))],
        out_specs=pl.BlockSpec((tm, LANES), lambda i: (i, 0)),
        out_shape=jax.ShapeDtypeStruct((n, LANES), F32),
        compiler_params=_params("parallel"),
        name="router",
    )(x2d, gain.reshape(1, d), r_pad)


def _final_norm_kernel(x_ref, g_ref, o_ref):
    x = x_ref[...]
    o_ref[...] = x * _rms_scale(x) * g_ref[...]


def final_norm(x2d, gain, *, tm=512):
    n, d = x2d.shape
    return pl.pallas_call(
        _final_norm_kernel,
        grid=(n // tm,),
        in_specs=[pl.BlockSpec((tm, d), lambda i: (i, 0)), pl.BlockSpec((1, d), lambda i: (0, 0))],
        out_specs=pl.BlockSpec((tm, d), lambda i: (i, 0)),
        out_shape=jax.ShapeDtypeStruct((n, d), F32),
        compiler_params=_params("parallel"),
        name="final_norm",
    )(x2d, gain.reshape(1, d))


def rope_tables(seq):
    half = HEAD_DIM // 2
    inv = jnp.power(jnp.float32(ROPE_THETA), -jnp.arange(half, dtype=F32) / half)
    ang = jnp.arange(seq).astype(F32)[:, None] * inv[None, :]
    cos, sin = jnp.cos(ang), jnp.sin(ang)
    return jnp.concatenate([cos, cos], axis=1), jnp.concatenate([-sin, sin], axis=1)


EVEN_MODES = (2, 1, 0, 3, 0, 0)
ODD_MODES = (2, 2, 1, 1, 0, 0)


def trunk(x, p):
    batch, seq, d = x.shape
    n = batch * seq
    cos, sin = rope_tables(seq)
    x2 = x.reshape(n, d)
    for i in range(DEPTH):
        j = i // 2
        if i % 2 == 0:
            qkv = norm_proj(x2, p["ev_norm_mix"][j], p["ev_w_in"][j], cos, sin, EVEN_MODES, seq)
            qkv3 = qkv.reshape(batch, seq, W_IN)
            oa = dilated_attention(qkv3, batch, seq)
            ob = neighbourhood_attention(qkv3, p["ev_bias"][j], batch, seq)
            x2 = proj_residual([oa.reshape(n, -1), ob.reshape(n, -1)], p["ev_w_out"][j], x2)
            x2 = swiglu_residual(x2, p["ev_norm_ffn"][j], p["ev_w_gate"][j:j + 1], p["ev_w_up"][j:j + 1],
                                 p["ev_w_down"][j:j + 1])
        else:
            qkv = norm_proj(x2, p["od_norm_mix"][j], p["od_w_qkv"][j], cos, sin, ODD_MODES, seq)
            o = diff_attention(qkv.reshape(batch, seq, W_IN), p["od_lambda_q1"][j], p["od_lambda_k1"][j],
                               p["od_lambda_q2"][j], p["od_lambda_k2"][j], p["od_subln"][j], i, batch, seq)
            x2 = proj_residual([o.reshape(n, -1)], p["od_w_out"][j], x2)
            gates = router_gates(x2, p["od_norm_ffn"][j], p["od_router"][j])
            x2 = swiglu_residual(x2, p["od_norm_ffn"][j], p["od_w_gate"][j], p["od_w_up"][j], p["od_w_down"][j],
                                 gates)
    return final_norm(x2, p["final_norm"]).reshape(batch, seq, d)


def kernel(x_prompt, x_sample, ev_norm_mix, ev_w_in, ev_w_out, ev_rpb, ev_norm_ffn, ev_w_gate, ev_w_up, ev_w_down, od_norm_mix, od_w_qkv, od_w_out, od_lambda_q1, od_lambda_k1, od_lambda_q2, od_lambda_k2, od_subln, od_norm_ffn, od_router, od_w_gate, od_w_up, od_w_down, final_norm):
    bf = lambda w: w.astype(BF16)
    p = dict(
        ev_norm_mix=ev_norm_mix, ev_w_in=bf(ev_w_in), ev_w_out=bf(ev_w_out),
        ev_bias=jax.vmap(na_bias_table)(ev_rpb), ev_norm_ffn=ev_norm_ffn,
        ev_w_gate=bf(ev_w_gate), ev_w_up=bf(ev_w_up), ev_w_down=bf(ev_w_down),
        od_norm_mix=od_norm_mix, od_w_qkv=bf(od_w_qkv), od_w_out=bf(od_w_out),
        od_lambda_q1=od_lambda_q1, od_lambda_k1=od_lambda_k1, od_lambda_q2=od_lambda_q2, od_lambda_k2=od_lambda_k2,
        od_subln=od_subln, od_norm_ffn=od_norm_ffn, od_router=od_router,
        od_w_gate=bf(od_w_gate), od_w_up=bf(od_w_up), od_w_down=bf(od_w_down), final_norm=final_norm,
    )
    return (trunk(x_prompt, p), trunk(x_sample, p))
```

```python
import functools
import math

import jax
import jax.numpy as jnp
from jax import lax
from jax.experimental import pallas as pl
from jax.experimental.pallas import tpu as pltpu

D_MODEL = 2048
DEPTH = 4
HEAD_DIM = 128
N_HEADS_A = 8
N_HEADS_B = 8
DILATED_PAIRS = ((128, 1), (512, 4), (2048, 16))
BAND_HALF = 64
GRID_W = 64
NA_KH = 8
NA_KW = 16
N_HEADS_C = 8
HEAD_DIM_C = 128
D_FF = 5632
N_EXPERTS = 8
TOP_K = 2
ROPE_THETA = 10000.0
EPS = 1e-6
W_IN = 3 * (N_HEADS_A + N_HEADS_B) * HEAD_DIM
SCALE = HEAD_DIM ** -0.5

LANES = 128
VMEM_LIMIT_BYTES = 56 * 1024 * 1024
NEG_BIG = -1e30

BF16 = jnp.bfloat16
F32 = jnp.float32
I32 = jnp.int32


def _params(*sem):
    return pltpu.CompilerParams(dimension_semantics=sem, vmem_limit_bytes=VMEM_LIMIT_BYTES)


def _rms_scale(x):
    return lax.rsqrt(jnp.mean(x * x, axis=-1, keepdims=True) + EPS)


def _norm_proj_kernel(x_ref, g_ref, w_ref, cos_ref, sin_ref, o_ref, h_ref, *, modes, tn):
    j = pl.program_id(1)

    @pl.when(j == 0)
    def _():
        x = x_ref[...]
        h_ref[...] = (x * _rms_scale(x) * g_ref[...]).astype(BF16)

    acc = jnp.dot(h_ref[...], w_ref[...], preferred_element_type=F32)

    def emit(rot, scale):
        for c in range(tn // LANES):
            blk = acc[:, c * LANES:(c + 1) * LANES]
            if rot:
                blk = blk * cos_ref[...] + pltpu.roll(blk, LANES // 2, 1) * sin_ref[...]
            if scale:
                blk = blk * SCALE
            o_ref[:, c * LANES:(c + 1) * LANES] = blk.astype(o_ref.dtype)

    for mode in sorted(set(modes)):
        tiles = [t for t, m in enumerate(modes) if m == mode]
        cond = functools.reduce(jnp.logical_or, [j == t for t in tiles])
        pl.when(cond)(functools.partial(emit, mode in (1, 2), mode in (2, 3)))


def norm_proj(x2d, gain, w, layer, cos, sin, modes, seq, *, tm=512, tn=1024):
    n, d = x2d.shape
    wout = w.shape[2]
    assert n % tm == 0 and wout % tn == 0 and seq % tm == 0 and len(modes) == wout // tn
    ns = seq // tm
    return pl.pallas_call(
        functools.partial(_norm_proj_kernel, modes=tuple(modes), tn=tn),
        grid=(n // tm, wout // tn),
        in_specs=[
            pl.BlockSpec((tm, d), lambda i, j: (i, 0)),
            pl.BlockSpec((None, 1, d), lambda i, j: (layer, 0, 0)),
            pl.BlockSpec((None, d, tn), lambda i, j: (layer, 0, j)),
            pl.BlockSpec((tm, LANES), lambda i, j: (i % ns, 0)),
            pl.BlockSpec((tm, LANES), lambda i, j: (i % ns, 0)),
        ],
        out_specs=pl.BlockSpec((tm, tn), lambda i, j: (i, j)),
        out_shape=jax.ShapeDtypeStruct((n, wout), BF16),
        scratch_shapes=[pltpu.VMEM((tm, d), BF16)],
        compiler_params=_params("parallel", "arbitrary"),
        name="norm_proj",
    )(x2d, gain, w, cos, sin)


def _band_kernel(*refs, length, first, last):
    if first:
        q_ref, k_ref, v_ref, o_ref, lse_ref = refs
    elif last:
        q_ref, k_ref, v_ref, op_ref, lp_ref, o_ref = refs
    else:
        q_ref, k_ref, v_ref, op_ref, lp_ref, o_ref, lse_ref = refs
    tq = LANES
    win = min(2 * tq, length)

    def body(t, carry):
        u0 = pl.multiple_of(t * tq, tq)
        ws = pl.multiple_of(jnp.clip(u0 - BAND_HALF, 0, length - win), BAND_HALF)
        q = q_ref[pl.ds(u0, tq), :]
        k = k_ref[pl.ds(ws, win), :]
        v = v_ref[pl.ds(ws, win), :]
        s = lax.dot_general(q, k, (((1,), (1,)), ((), ())), preferred_element_type=F32)
        qi = u0 + lax.broadcasted_iota(jnp.int32, (tq, win), 0)
        kj = ws + lax.broadcasted_iota(jnp.int32, (tq, win), 1)
        s = jnp.where(jnp.abs(qi - kj) <= BAND_HALF, s, NEG_BIG)
        m = jnp.max(s, axis=1, keepdims=True)
        p = jnp.exp(s - m)
        l = jnp.sum(p, axis=1, keepdims=True)
        o = jnp.dot(p.astype(BF16), v, preferred_element_type=F32) / l
        lse = jnp.broadcast_to(m + jnp.log(l), (tq, LANES))
        if not first:
            lse_p = lp_ref[pl.ds(u0, tq), :]
            o_p = op_ref[pl.ds(u0, tq), :]
            mx = jnp.maximum(lse_p, lse)
            e_p = jnp.exp(lse_p - mx)
            e_c = jnp.exp(lse - mx)
            tot = e_p + e_c
            o = (o_p * e_p + o * e_c) / tot
            lse = mx + jnp.log(tot)
        o_ref[pl.ds(u0, tq), :] = o.astype(o_ref.dtype)
        if not last:
            lse_ref[pl.ds(u0, tq), :] = lse
        return carry

    lax.fori_loop(0, length // tq, body, 0)


def dilated_attention(qkv, batch, seq):
    ncol = W_IN // LANES
    wa = N_HEADS_A * HEAD_DIM
    state = None
    for gi, (w, d) in enumerate(DILATED_PAIRS):
        assert (w // 2) // d == BAND_HALF and seq % (d * LANES) == 0
        first, last = gi == 0, gi == len(DILATED_PAIRS) - 1
        length = seq // d
        qkv_v = qkv.reshape(batch, length, d * W_IN)
        col = lambda off: pl.BlockSpec((None, length, LANES), lambda b, r, h, off=off: (b, 0, r * ncol + off + h))
        st_spec = pl.BlockSpec((None, length, LANES), lambda b, r, h: (b, 0, r * N_HEADS_A + h))
        in_specs = [col(0), col(N_HEADS_A), col(2 * N_HEADS_A)]
        args = [qkv_v, qkv_v, qkv_v]
        if not first:
            in_specs += [st_spec, st_spec]
            args += [state[0].reshape(batch, length, d * wa), state[1].reshape(batch, length, d * wa)]
        o_shape = jax.ShapeDtypeStruct((batch, length, d * wa), BF16 if last else F32)
        lse_shape = jax.ShapeDtypeStruct((batch, length, d * wa), F32)
        out = pl.pallas_call(
            functools.partial(_band_kernel, length=length, first=first, last=last),
            grid=(batch, d, N_HEADS_A),
            in_specs=in_specs,
            out_specs=st_spec if last else [st_spec, st_spec],
            out_shape=o_shape if last else [o_shape, lse_shape],
            compiler_params=_params("parallel", "parallel", "parallel"),
            name=f"band_attn_d{d}",
        )(*args)
        state = out
    return state.reshape(batch, seq, wa)


def na_bias_table(rpb):
    cols = jnp.arange(GRID_W)
    cs = jnp.clip(cols - NA_KW // 2, 0, GRID_W - NA_KW)
    col_mask = (cols[None, :] >= cs[:, None]) & (cols[None, :] < cs[:, None] + NA_KW)
    coff = jnp.clip(cols[None, :] - cols[:, None], -(NA_KW - 1), NA_KW - 1) + (NA_KW - 1)
    roff = jnp.arange(NA_KH)[None, :] - jnp.arange(NA_KH)[:, None] + (NA_KH - 1)
    row_hot = jax.nn.one_hot(roff, 2 * NA_KH - 1, dtype=F32)
    col_hot = jax.nn.one_hot(coff, 2 * NA_KW - 1, dtype=F32)
    bias = jnp.einsum("hrc,djr,qkc->hdqjk", rpb.astype(F32), row_hot, col_hot, precision=lax.Precision.HIGHEST)
    bias = jnp.where(col_mask[None, None, :, None, :], bias, NEG_BIG)
    return bias.reshape(rpb.shape[0], NA_KH, GRID_W, NA_KH * GRID_W)


def _na_kernel(q_ref, k_ref, v_ref, b_ref, o_ref, *, rows):
    nk = NA_KH * GRID_W

    def body(r, carry):
        rs = jnp.clip(r - NA_KH // 2, 0, rows - NA_KH)
        q = q_ref[pl.ds(pl.multiple_of(r * GRID_W, GRID_W), GRID_W), :]
        k0 = pl.multiple_of(rs * GRID_W, GRID_W)
        k = k_ref[pl.ds(k0, nk), :]
        v = v_ref[pl.ds(k0, nk), :]
        s = lax.dot_general(q, k, (((1,), (1,)), ((), ())), preferred_element_type=F32)
        s = s + b_ref[r - rs]
        m = jnp.max(s, axis=1, keepdims=True)
        p = jnp.exp(s - m)
        l = jnp.sum(p, axis=1, keepdims=True)
        o = jnp.dot(p.astype(BF16), v, preferred_element_type=F32) / l
        o_ref[pl.ds(pl.multiple_of(r * GRID_W, GRID_W), GRID_W), :] = o.astype(o_ref.dtype)
        return carry

    lax.fori_loop(0, rows, body, 0)


def neighbourhood_attention(qkv, bias, layer, batch, seq):
    rows = seq // GRID_W
    assert rows >= NA_KH and seq % GRID_W == 0
    base = 3 * N_HEADS_A
    col = lambda off: pl.BlockSpec((None, seq, LANES), lambda b, h, off=off: (b, 0, base + off + h))
    return pl.pallas_call(
        functools.partial(_na_kernel, rows=rows),
        grid=(batch, N_HEADS_B),
        in_specs=[col(0), col(N_HEADS_B), col(2 * N_HEADS_B),
                  pl.BlockSpec((None, None, NA_KH, GRID_W, NA_KH * GRID_W), lambda b, h: (layer, h, 0, 0, 0))],
        out_specs=pl.BlockSpec((None, seq, LANES), lambda b, h: (b, 0, h)),
        out_shape=jax.ShapeDtypeStruct((batch, seq, N_HEADS_B * HEAD_DIM), BF16),
        compiler_params=_params("parallel", "parallel"),
        name="na_attn",
    )(qkv, qkv, qkv, bias)


def _diff_kernel(q_ref, k_ref, v_ref, lq1_ref, lk1_ref, lq2_ref, lk2_ref, g_ref, o_ref, *, lam_init):
    lam = (jnp.exp(jnp.sum(lq1_ref[...] * lk1_ref[...], axis=1, keepdims=True))
           - jnp.exp(jnp.sum(lq2_ref[...] * lk2_ref[...], axis=1, keepdims=True)) + lam_init)
    v = v_ref[...]

    def half(c):
        q = q_ref[:, c * LANES:(c + 1) * LANES]
        k = k_ref[:, c * LANES:(c + 1) * LANES]
        s = lax.dot_general(q, k, (((1,), (1,)), ((), ())), preferred_element_type=F32)
        m = jnp.max(s, axis=1, keepdims=True)
        p = jnp.exp(s - m)
        l = jnp.sum(p, axis=1, keepdims=True)
        return jnp.dot(p.astype(BF16), v, preferred_element_type=F32) / l

    o = half(0) - lam * half(1)
    o = o * _rms_scale(o) * g_ref[...] * (1.0 - lam_init)
    o_ref[...] = o.astype(o_ref.dtype)


def diff_attention(qkv, lq1, lk1, lq2, lk2, subln, layer, layer_idx, batch, seq, *, tq=256):
    hw = 2 * HEAD_DIM_C
    lam_init = 0.8 - 0.6 * math.exp(-0.3 * layer_idx)
    vec = lambda n: pl.BlockSpec((None, 1, n), lambda b, h, t: (layer, 0, 0))
    return pl.pallas_call(
        functools.partial(_diff_kernel, lam_init=lam_init),
        grid=(batch, N_HEADS_C, seq // tq),
        in_specs=[pl.BlockSpec((None, tq, hw), lambda b, h, t: (b, t, h)),
                  pl.BlockSpec((None, seq, hw), lambda b, h, t: (b, 0, N_HEADS_C + h)),
                  pl.BlockSpec((None, seq, hw), lambda b, h, t: (b, 0, 2 * N_HEADS_C + h)),
                  vec(HEAD_DIM_C), vec(HEAD_DIM_C), vec(HEAD_DIM_C), vec(HEAD_DIM_C), vec(hw)],
        out_specs=pl.BlockSpec((None, tq, hw), lambda b, h, t: (b, t, h)),
        out_shape=jax.ShapeDtypeStruct((batch, seq, N_HEADS_C * hw), BF16),
        compiler_params=_params("parallel", "parallel", "arbitrary"),
        name="diff_attn",
    )(qkv, qkv, qkv, lq1, lk1, lq2, lk2, subln)


def _proj_res_kernel(*refs, n_in):
    a_refs, w_ref, x_ref, o_ref = refs[:n_in], refs[n_in], refs[n_in + 1], refs[n_in + 2]
    acc = x_ref[...]
    k0 = 0
    for a_ref in a_refs:
        kk = a_ref.shape[1]
        acc = acc + jnp.dot(a_ref[...], w_ref[k0:k0 + kk, :], preferred_element_type=F32)
        k0 += kk
    o_ref[...] = acc


def proj_residual(acts, w, layer, x2d, *, tm=512, tn=1024):
    n, d = x2d.shape
    ktot = w.shape[1]
    assert sum(a.shape[1] for a in acts) == ktot and n % tm == 0 and d % tn == 0
    return pl.pallas_call(
        functools.partial(_proj_res_kernel, n_in=len(acts)),
        grid=(n // tm, d // tn),
        in_specs=[pl.BlockSpec((tm, a.shape[1]), lambda i, j: (i, 0)) for a in acts]
        + [pl.BlockSpec((None, ktot, tn), lambda i, j: (layer, 0, j)), pl.BlockSpec((tm, tn), lambda i, j: (i, j))],
        out_specs=pl.BlockSpec((tm, tn), lambda i, j: (i, j)),
        out_shape=jax.ShapeDtypeStruct((n, d), F32),
        compiler_params=_params("parallel", "parallel"),
        name="proj_residual",
    )(*acts, w, x2d)


def _swiglu_kernel(x_ref, g_ref, wg_ref, wu_ref, wd_ref, o_ref, h_ref, acc_ref):
    f = pl.program_id(1)

    @pl.when(f == 0)
    def _():
        x = x_ref[...]
        h_ref[...] = (x * _rms_scale(x) * g_ref[...]).astype(BF16)
        acc_ref[...] = x

    h = h_ref[...]
    gt = jnp.dot(h, wg_ref[...], preferred_element_type=F32)
    up = jnp.dot(h, wu_ref[...], preferred_element_type=F32)
    a = gt * jax.nn.sigmoid(gt) * up
    acc_ref[...] += jnp.dot(a.astype(BF16), wd_ref[...], preferred_element_type=F32)

    @pl.when(f == pl.num_programs(1) - 1)
    def _():
        o_ref[...] = acc_ref[...]


def swiglu_residual(x2d, gain, wg, wu, wd, layer, *, tm=512, tf=512):
    n, d = x2d.shape
    ff = wg.shape[2]
    assert n % tm == 0 and ff % tf == 0
    return pl.pallas_call(
        _swiglu_kernel,
        grid=(n // tm, ff // tf),
        in_specs=[pl.BlockSpec((tm, d), lambda i, f: (i, 0)),
                  pl.BlockSpec((None, 1, d), lambda i, f: (layer, 0, 0)),
                  pl.BlockSpec((None, d, tf), lambda i, f: (layer, 0, f)),
                  pl.BlockSpec((None, d, tf), lambda i, f: (layer, 0, f)),
                  pl.BlockSpec((None, tf, d), lambda i, f: (layer, f, 0))],
        out_specs=pl.BlockSpec((tm, d), lambda i, f: (i, 0)),
        out_shape=jax.ShapeDtypeStruct((n, d), F32),
        scratch_shapes=[pltpu.VMEM((tm, d), BF16), pltpu.VMEM((tm, d), F32)],
        compiler_params=_params("parallel", "arbitrary"),
        name="swiglu",
    )(x2d, gain, wg, wu, wd)


SEL_E1, SEL_E2, SEL_R1, SEL_R2 = 0, 1, 2, 3


def _router_kernel(x_ref, g_ref, r_ref, sel_ref, gate_ref, cnt_ref, tri_ref):
    tm = x_ref.shape[0]

    @pl.when(pl.program_id(0) == 0)
    def _():
        cnt_ref[...] = jnp.zeros_like(cnt_ref)
        row = lax.broadcasted_iota(I32, (tm, tm), 0)
        col = lax.broadcasted_iota(I32, (tm, tm), 1)
        tri_ref[...] = jnp.where(col < row, 1.0, 0.0).astype(BF16)

    x = x_ref[...]
    h = x * _rms_scale(x) * g_ref[...]
    logits = jnp.dot(h, r_ref[...], preferred_element_type=F32, precision=lax.Precision.HIGHEST)
    lane = lax.broadcasted_iota(I32, logits.shape, 1)
    logits = jnp.where(lane < N_EXPERTS, logits, -jnp.inf)
    m1 = jnp.max(logits, axis=1, keepdims=True)
    i1 = jnp.min(jnp.where(logits == m1, lane, LANES), axis=1, keepdims=True)
    rest = jnp.where(lane == i1, -jnp.inf, logits)
    m2 = jnp.max(rest, axis=1, keepdims=True)
    i2 = jnp.min(jnp.where(rest == m2, lane, LANES), axis=1, keepdims=True)
    e2 = jnp.exp(m2 - m1)
    g1 = 1.0 / (1.0 + e2)
    g2 = e2 / (1.0 + e2)
    hot = jnp.where((lane == i1) | (lane == i2), 1.0, 0.0)
    rank = jnp.dot(tri_ref[...], hot.astype(BF16), preferred_element_type=F32) + cnt_ref[...]
    r1 = jnp.sum(jnp.where(lane == i1, rank, 0.0), axis=1, keepdims=True).astype(I32)
    r2 = jnp.sum(jnp.where(lane == i2, rank, 0.0), axis=1, keepdims=True).astype(I32)
    sel = jnp.where(lane == SEL_E1, i1, 0) + jnp.where(lane == SEL_E2, i2, 0)
    sel = sel + jnp.where(lane == SEL_R1, r1, 0) + jnp.where(lane == SEL_R2, r2, 0)
    sel_ref[...] = sel
    gate_ref[...] = jnp.where(lane == 0, g1, 0.0) + jnp.where(lane == 1, g2, 0.0)
    cnt_ref[...] += jnp.sum(hot, axis=0, keepdims=True)


def router(x2d, gain, r_pad, layer, *, tm=512):
    n, d = x2d.shape
    return pl.pallas_call(
        _router_kernel,
        grid=(n // tm,),
        in_specs=[pl.BlockSpec((tm, d), lambda i: (i, 0)), pl.BlockSpec((None, 1, d), lambda i: (layer, 0, 0)),
                  pl.BlockSpec((None, d, LANES), lambda i: (layer, 0, 0))],
        out_specs=[pl.BlockSpec((tm, LANES), lambda i: (i, 0)), pl.BlockSpec((tm, LANES), lambda i: (i, 0)),
                   pl.BlockSpec((1, LANES), lambda i: (0, 0))],
        out_shape=[jax.ShapeDtypeStruct((n, LANES), I32), jax.ShapeDtypeStruct((n, LANES), F32),
                   jax.ShapeDtypeStruct((1, LANES), F32)],
        scratch_shapes=[pltpu.VMEM((tm, tm), BF16)],
        compiler_params=_params("arbitrary"),
        name="router",
    )(x2d, gain, r_pad)


def _row_copies(src_of, dst_of, sem, rows, wait):
    def body(r, carry):
        cp = pltpu.make_async_copy(src_of(r), dst_of(r), sem)
        if wait:
            cp.wait()
        else:
            cp.start()
        return carry
    lax.fori_loop(0, rows, body, 0, unroll=8)


def _dispatch_kernel(p1_ref, p2_ref, x_ref, g_ref, hs_ref, hbuf, sem):
    tm = x_ref.shape[0]
    x = x_ref[...]
    hbuf[...] = x * _rms_scale(x) * g_ref[...]
    src = lambda r: hbuf.at[pl.ds(r, 1)]
    for p_ref in (p1_ref, p2_ref):
        _row_copies(src, lambda r, p_ref=p_ref: hs_ref.at[pl.ds(p_ref[0, r], 1)], sem, tm, wait=False)
    for p_ref in (p1_ref, p2_ref):
        _row_copies(src, lambda r, p_ref=p_ref: hs_ref.at[pl.ds(p_ref[0, r], 1)], sem, tm, wait=True)


def dispatch(x2d, gain, layer, pos1, pos2, *, tm=256):
    n, d = x2d.shape
    pspec = pl.BlockSpec((None, 1, tm), lambda i: (i, 0, 0), memory_space=pltpu.SMEM)
    return pl.pallas_call(
        _dispatch_kernel,
        grid=(n // tm,),
        in_specs=[pspec, pspec, pl.BlockSpec((tm, d), lambda i: (i, 0)),
                  pl.BlockSpec((None, 1, d), lambda i: (layer, 0, 0))],
        out_specs=pl.BlockSpec(memory_space=pl.ANY),
        out_shape=jax.ShapeDtypeStruct((TOP_K * n, d), F32),
        scratch_shapes=[pltpu.VMEM((tm, d), F32), pltpu.SemaphoreType.DMA(())],
        compiler_params=_params("arbitrary"),
        name="moe_dispatch",
    )(pos1.reshape(n // tm, 1, tm), pos2.reshape(n // tm, 1, tm), x2d, gain)


VISIT_VALID, VISIT_FIRST = 1, 2


def _grouped_swiglu_kernel(tile_ref, exp_ref, flag_ref, offs_ref, hs_ref, wg_ref, wu_ref, wd_ref, o_ref, h_ref, acc_ref):
    v, f = pl.program_id(0), pl.program_id(1)
    tm = hs_ref.shape[0]
    flags = flag_ref[v]

    @pl.when((flags & VISIT_VALID) != 0)
    def _():
        @pl.when(f == 0)
        def _():
            h_ref[...] = hs_ref[...].astype(BF16)
            acc_ref[...] = jnp.zeros_like(acc_ref)

        h = h_ref[...]
        gt = jnp.dot(h, wg_ref[...], preferred_element_type=F32)
        up = jnp.dot(h, wu_ref[...], preferred_element_type=F32)
        a = gt * jax.nn.sigmoid(gt) * up
        acc_ref[...] += jnp.dot(a.astype(BF16), wd_ref[...], preferred_element_type=F32)

        @pl.when(f == pl.num_programs(1) - 1)
        def _():
            e = exp_ref[v]
            row = tile_ref[v] * tm + lax.broadcasted_iota(I32, (tm, 1), 0)
            mine = (row >= offs_ref[e]) & (row < offs_ref[e + 1])

            @pl.when((flags & VISIT_FIRST) != 0)
            def _():
                o_ref[...] = jnp.where(mine, acc_ref[...], 0.0)

            @pl.when((flags & VISIT_FIRST) == 0)
            def _():
                o_ref[...] = jnp.where(mine, acc_ref[...], o_ref[...])


def grouped_swiglu(hs, wg, wu, wd, layer, tile_ids, exp_ids, flags, offs, *, tm=512, tf=512):
    ns, d = hs.shape
    ff = wg.shape[3]
    nv = tile_ids.shape[0]
    grid_spec = pltpu.PrefetchScalarGridSpec(
        num_scalar_prefetch=4,
        grid=(nv, ff // tf),
        in_specs=[pl.BlockSpec((tm, d), lambda v, f, t, e, fl, o: (t[v], 0)),
                  pl.BlockSpec((None, None, d, tf), lambda v, f, t, e, fl, o: (layer, e[v], 0, f)),
                  pl.BlockSpec((None, None, d, tf), lambda v, f, t, e, fl, o: (layer, e[v], 0, f)),
                  pl.BlockSpec((None, None, tf, d), lambda v, f, t, e, fl, o: (layer, e[v], f, 0))],
        out_specs=pl.BlockSpec((tm, d), lambda v, f, t, e, fl, o: (t[v], 0)),
        scratch_shapes=[pltpu.VMEM((tm, d), BF16), pltpu.VMEM((tm, d), F32)],
    )
    return pl.pallas_call(
        _grouped_swiglu_kernel,
        grid_spec=grid_spec,
        out_shape=jax.ShapeDtypeStruct((ns, d), F32),
        compiler_params=_params("arbitrary", "arbitrary"),
        name="moe_grouped_swiglu",
    )(tile_ids, exp_ids, flags, offs, hs, wg, wu, wd)


def _combine_kernel(p1_ref, p2_ref, x_ref, gate_ref, ys_ref, o_ref, buf1, buf2, sem):
    tm = x_ref.shape[0]
    for p_ref, buf in ((p1_ref, buf1), (p2_ref, buf2)):
        _row_copies(lambda r, p_ref=p_ref: ys_ref.at[pl.ds(p_ref[0, r], 1)], lambda r, buf=buf: buf.at[pl.ds(r, 1)],
                    sem, tm, wait=False)
    for p_ref, buf in ((p1_ref, buf1), (p2_ref, buf2)):
        _row_copies(lambda r, p_ref=p_ref: ys_ref.at[pl.ds(p_ref[0, r], 1)], lambda r, buf=buf: buf.at[pl.ds(r, 1)],
                    sem, tm, wait=True)
    gate = gate_ref[...]
    o_ref[...] = x_ref[...] + gate[:, 0:1] * buf1[...] + gate[:, 1:2] * buf2[...]


def combine(x2d, gates, ys, pos1, pos2, *, tm=256):
    n, d = x2d.shape
    pspec = pl.BlockSpec((None, 1, tm), lambda i: (i, 0, 0), memory_space=pltpu.SMEM)
    return pl.pallas_call(
        _combine_kernel,
        grid=(n // tm,),
        in_specs=[pspec, pspec, pl.BlockSpec((tm, d), lambda i: (i, 0)), pl.BlockSpec((tm, LANES), lambda i: (i, 0)),
                  pl.BlockSpec(memory_space=pl.ANY)],
        out_specs=pl.BlockSpec((tm, d), lambda i: (i, 0)),
        out_shape=jax.ShapeDtypeStruct((n, d), F32),
        scratch_shapes=[pltpu.VMEM((tm, d), F32), pltpu.VMEM((tm, d), F32), pltpu.SemaphoreType.DMA(())],
        compiler_params=_params("arbitrary"),
        name="moe_combine",
    )(pos1.reshape(n // tm, 1, tm), pos2.reshape(n // tm, 1, tm), x2d, gates, ys)


def routing_tables(sel, counts, n_slots, tm):
    cnt = counts[0, :N_EXPERTS].astype(I32)
    offs = jnp.concatenate([jnp.zeros((1,), I32), jnp.cumsum(cnt)])
    experts = jnp.arange(N_EXPERTS, dtype=I32)
    off_of = lambda e: jnp.sum(jnp.where(e[:, None] == experts[None, :], offs[None, :N_EXPERTS], 0), axis=1)
    pos1 = off_of(sel[:, SEL_E1]) + sel[:, SEL_R1]
    pos2 = off_of(sel[:, SEL_E2]) + sel[:, SEL_R2]
    n_tiles = n_slots // tm
    n_visits = n_tiles + N_EXPERTS - 1
    first_tile = offs[:-1] // tm
    tiles_of = jnp.where(cnt > 0, (offs[1:] - 1) // tm - first_tile + 1, 0)
    v_end = jnp.cumsum(tiles_of)
    v_start = v_end - tiles_of
    v = jnp.arange(n_visits, dtype=I32)
    valid = v < v_end[-1]
    e_of = jnp.minimum(jnp.sum((v[:, None] >= v_end[None, :]).astype(I32), axis=1), N_EXPERTS - 1)
    pick = lambda table: jnp.sum(jnp.where(e_of[:, None] == experts[None, :], table[None, :], 0), axis=1)
    tile = pick(first_tile) + v - pick(v_start)
    last_e = jnp.max(jnp.where(cnt > 0, experts, 0))
    tile = jnp.where(valid, tile, n_tiles - 1).astype(I32)
    e_of = jnp.where(valid, e_of, last_e).astype(I32)
    prev_tile = jnp.concatenate([jnp.full((1,), -1, I32), tile[:-1]])
    flags = jnp.where(valid, VISIT_VALID, 0) + jnp.where(tile != prev_tile, VISIT_FIRST, 0)
    return pos1, pos2, tile, e_of, flags.astype(I32), offs


def moe_residual(x2d, p, layer, *, tm=512):
    n, d = x2d.shape
    sel, gates, counts = router(x2d, p["od_norm_ffn"], p["od_router"], layer)
    pos1, pos2, tile, e_of, flags, offs = routing_tables(sel, counts, TOP_K * n, tm)
    hs = dispatch(x2d, p["od_norm_ffn"], layer, pos1, pos2)
    ys = grouped_swiglu(hs, p["od_w_gate"], p["od_w_up"], p["od_w_down"], layer, tile, e_of, flags, offs, tm=tm)
    return combine(x2d, gates, ys, pos1, pos2)


def _final_norm_kernel(x_ref, g_ref, o_ref):
    x = x_ref[...]
    o_ref[...] = x * _rms_scale(x) * g_ref[...]


def final_norm(x2d, gain, *, tm=512):
    n, d = x2d.shape
    return pl.pallas_call(
        _final_norm_kernel,
        grid=(n // tm,),
        in_specs=[pl.BlockSpec((tm, d), lambda i: (i, 0)), pl.BlockSpec((1, d), lambda i: (0, 0))],
        out_specs=pl.BlockSpec((tm, d), lambda i: (i, 0)),
        out_shape=jax.ShapeDtypeStruct((n, d), F32),
        compiler_params=_params("parallel"),
        name="final_norm",
    )(x2d, gain.reshape(1, d))


def rope_tables(seq):
    half = HEAD_DIM // 2
    inv = jnp.power(jnp.float32(ROPE_THETA), -jnp.arange(half, dtype=F32) / half)
    ang = jnp.arange(seq).astype(F32)[:, None] * inv[None, :]
    cos, sin = jnp.cos(ang), jnp.sin(ang)
    return jnp.concatenate([cos, cos], axis=1), jnp.concatenate([-sin, sin], axis=1)


EVEN_MODES = (2, 1, 0, 3, 0, 0)
ODD_MODES = (2, 2, 1, 1, 0, 0)


def trunk(x, p):
    batch, seq, d = x.shape
    n = batch * seq
    cos, sin = rope_tables(seq)
    x2 = x.reshape(n, d)
    for i in range(DEPTH):
        j = i // 2
        if i % 2 == 0:
            qkv = norm_proj(x2, p["ev_norm_mix"], p["ev_w_in"], j, cos, sin, EVEN_MODES, seq)
            qkv3 = qkv.reshape(batch, seq, W_IN)
            oa = dilated_attention(qkv3, batch, seq)
            ob = neighbourhood_attention(qkv3, p["ev_bias"], j, batch, seq)
            x2 = proj_residual([oa.reshape(n, -1), ob.reshape(n, -1)], p["ev_w_out"], j, x2)
            x2 = swiglu_residual(x2, p["ev_norm_ffn"], p["ev_w_gate"], p["ev_w_up"], p["ev_w_down"], j)
        else:
            qkv = norm_proj(x2, p["od_norm_mix"], p["od_w_qkv"], j, cos, sin, ODD_MODES, seq)
            o = diff_attention(qkv.reshape(batch, seq, W_IN), p["od_lambda_q1"], p["od_lambda_k1"],
                               p["od_lambda_q2"], p["od_lambda_k2"], p["od_subln"], j, i, batch, seq)
            x2 = proj_residual([o.reshape(n, -1)], p["od_w_out"], j, x2)
            x2 = moe_residual(x2, p, j)
    return final_norm(x2, p["final_norm"]).reshape(batch, seq, d)


def kernel(x_prompt, x_sample, ev_norm_mix, ev_w_in, ev_w_out, ev_rpb, ev_norm_ffn, ev_w_gate, ev_w_up, ev_w_down, od_norm_mix, od_w_qkv, od_w_out, od_lambda_q1, od_lambda_k1, od_lambda_q2, od_lambda_k2, od_subln, od_norm_ffn, od_router, od_w_gate, od_w_up, od_w_down, final_norm):
    bf = lambda w: w.astype(BF16)
    r_pad = jnp.pad(od_router.astype(F32), ((0, 0), (0, 0), (0, LANES - N_EXPERTS)))
    vec = lambda t: t.reshape(t.shape[0], 1, t.shape[1])
    p = dict(
        ev_norm_mix=vec(ev_norm_mix), ev_w_in=bf(ev_w_in), ev_w_out=bf(ev_w_out),
        ev_bias=jax.vmap(na_bias_table)(ev_rpb), ev_norm_ffn=vec(ev_norm_ffn),
        ev_w_gate=bf(ev_w_gate), ev_w_up=bf(ev_w_up), ev_w_down=bf(ev_w_down),
        od_norm_mix=vec(od_norm_mix), od_w_qkv=bf(od_w_qkv), od_w_out=bf(od_w_out),
        od_lambda_q1=vec(od_lambda_q1), od_lambda_k1=vec(od_lambda_k1), od_lambda_q2=vec(od_lambda_q2),
        od_lambda_k2=vec(od_lambda_k2), od_subln=vec(od_subln), od_norm_ffn=vec(od_norm_ffn), od_router=r_pad,
        od_w_gate=bf(od_w_gate), od_w_up=bf(od_w_up), od_w_down=bf(od_w_down), final_norm=final_norm,
    )
    return (trunk(x_prompt, p), trunk(x_sample, p))
```

```python
import functools
import math

import jax
import jax.numpy as jnp
from jax import lax
from jax.experimental import pallas as pl
from jax.experimental.pallas import tpu as pltpu

D_MODEL = 2048
DEPTH = 4
HEAD_DIM = 128
N_HEADS_A = 8
N_HEADS_B = 8
DILATED_PAIRS = ((128, 1), (512, 4), (2048, 16))
BAND_HALF = 64
GRID_W = 64
NA_KH = 8
NA_KW = 16
N_HEADS_C = 8
HEAD_DIM_C = 128
D_FF = 5632
N_EXPERTS = 8
TOP_K = 2
ROPE_THETA = 10000.0
EPS = 1e-6
W_IN = 3 * (N_HEADS_A + N_HEADS_B) * HEAD_DIM
SCALE = HEAD_DIM ** -0.5
LOG2E = math.log2(math.e)
QSCALE = SCALE * LOG2E

LANES = 128
VMEM_LIMIT_BYTES = 56 * 1024 * 1024
NEG_BIG = -1e30

BF16 = jnp.bfloat16
F32 = jnp.float32
I32 = jnp.int32


def _params(*sem):
    return pltpu.CompilerParams(dimension_semantics=sem, vmem_limit_bytes=VMEM_LIMIT_BYTES)


def _rms_scale(x):
    return lax.rsqrt(jnp.mean(x * x, axis=-1, keepdims=True) + EPS)


def _norm_proj_kernel(x_ref, g_ref, w_ref, cos_ref, sin_ref, o_ref, h_ref, *, modes, tn):
    j = pl.program_id(1)

    @pl.when(j == 0)
    def _():
        x = x_ref[...]
        h_ref[...] = (x * _rms_scale(x) * g_ref[...]).astype(BF16)

    acc = jnp.dot(h_ref[...], w_ref[...], preferred_element_type=F32)

    def emit(rot, scale):
        for c in range(tn // LANES):
            blk = acc[:, c * LANES:(c + 1) * LANES]
            if rot:
                blk = blk * cos_ref[...] + pltpu.roll(blk, LANES // 2, 1) * sin_ref[...]
            if scale:
                blk = blk * QSCALE
            o_ref[:, c * LANES:(c + 1) * LANES] = blk.astype(o_ref.dtype)

    for mode in sorted(set(modes)):
        tiles = [t for t, m in enumerate(modes) if m == mode]
        cond = functools.reduce(jnp.logical_or, [j == t for t in tiles])
        pl.when(cond)(functools.partial(emit, mode in (1, 2), mode in (2, 3)))


def norm_proj(x2d, gain, w, layer, cos, sin, modes, seq, *, tm=1024, tn=1024):
    n, d = x2d.shape
    wout = w.shape[2]
    assert n % tm == 0 and wout % tn == 0 and seq % tm == 0 and len(modes) == wout // tn
    ns = seq // tm
    return pl.pallas_call(
        functools.partial(_norm_proj_kernel, modes=tuple(modes), tn=tn),
        grid=(n // tm, wout // tn),
        in_specs=[
            pl.BlockSpec((tm, d), lambda i, j: (i, 0)),
            pl.BlockSpec((None, 1, d), lambda i, j: (layer, 0, 0)),
            pl.BlockSpec((None, d, tn), lambda i, j: (layer, 0, j)),
            pl.BlockSpec((tm, LANES), lambda i, j: (i % ns, 0)),
            pl.BlockSpec((tm, LANES), lambda i, j: (i % ns, 0)),
        ],
        out_specs=pl.BlockSpec((tm, tn), lambda i, j: (i, j)),
        out_shape=jax.ShapeDtypeStruct((n, wout), BF16),
        scratch_shapes=[pltpu.VMEM((tm, d), BF16)],
        compiler_params=_params("parallel", "arbitrary"),
        name="norm_proj",
    )(x2d, gain, w, cos, sin)


BAND_GROUP = 4


def _regroup(src, dst, n_prev, factor, seq, cast_dst=None):
    lp = seq // n_prev
    ln = lp // factor
    for c in range(n_prev):
        for r in range(factor):
            x = src[pl.ds(c * lp + r, ln, stride=factor), :]
            dense = pl.ds(c * lp + r * ln, ln)
            if dst is not None:
                dst[dense, :] = x
            if cast_dst is not None:
                cast_dst[dense, :] = x.astype(BF16)


def _ungroup(src, dst, n_prev, factor, seq):
    lp = seq // n_prev
    ln = lp // factor
    for c in range(n_prev):
        for r in range(factor):
            dst[pl.ds(c * lp + r, ln, stride=factor), :] = src[pl.ds(c * lp + r * ln, ln), :]


def _dilated_kernel(q_ref, k_ref, v_ref, o_ref, qa, ka, va, qb, kb, vb, qc, kc, vc, oa, la, ob, lb, *, seq):
    tq = LANES
    dils = [d for _, d in DILATED_PAIRS]
    n_pairs = len(dils)
    qkv_src, qkv_dst = (qa, ka, va), (qb, kb, vb)
    st_src, st_dst = (oa, la), (ob, lb)
    for ref, f32 in zip((q_ref, k_ref, v_ref), qkv_src):
        f32[...] = ref[...].astype(F32)
    for gi, d in enumerate(dils):
        first, last = gi == 0, gi == n_pairs - 1
        length = seq // d
        win = min(2 * tq, length)
        if first:
            assert d == 1
            qs, ks, vs = q_ref, k_ref, v_ref
        else:
            factor = d // dils[gi - 1]
            for src, dst, cast in zip(qkv_src, qkv_dst, (qc, kc, vc)):
                _regroup(src, None if last else dst, dils[gi - 1], factor, seq, cast_dst=cast)
            for src, dst in zip(st_src, st_dst):
                _regroup(src, dst, dils[gi - 1], factor, seq)
            qkv_src, qkv_dst = qkv_dst, qkv_src
            st_src, st_dst = st_dst, st_src
            qs, ks, vs = qc, kc, vc
        os_, ls_ = st_src
        diff0 = (lax.broadcasted_iota(I32, (tq, win), 0) - lax.broadcasted_iota(I32, (tq, win), 1))

        def body(g, carry, qs=qs, ks=ks, vs=vs, os_=os_, ls_=ls_, length=length, win=win, first=first, last=last,
                 diff0=diff0):
            idx = range(BAND_GROUP)
            u0 = [pl.multiple_of((g * BAND_GROUP + i) * tq, tq) for i in idx]
            c0 = [((g * BAND_GROUP + i) // (length // tq)) * length for i in idx]
            ws = [pl.multiple_of(jnp.clip(u0[i] - BAND_HALF, c0[i], c0[i] + length - win), BAND_HALF) for i in idx]
            s = [lax.dot_general(qs[pl.ds(u0[i], tq), :], ks[pl.ds(ws[i], win), :], (((1,), (1,)), ((), ())),
                                 preferred_element_type=F32) for i in idx]
            s = [jnp.where(jnp.abs(diff0 + (u0[i] - ws[i])) <= BAND_HALF, s[i], NEG_BIG) for i in idx]
            m = [jnp.max(x, axis=1, keepdims=True) for x in s]
            p = [jnp.exp2(s[i] - m[i]) for i in idx]
            l = [jnp.sum(x, axis=1, keepdims=True) for x in p]
            o = [jnp.dot(p[i].astype(BF16), vs[pl.ds(ws[i], win), :], preferred_element_type=F32) * (1.0 / l[i])
                 for i in idx]
            lse = [jnp.broadcast_to(m[i] + jnp.log2(l[i]), (tq, LANES)) for i in idx]
            if not first:
                lse_p = [ls_[pl.ds(u0[i], tq), :] for i in idx]
                mx = [jnp.maximum(lse_p[i], lse[i]) for i in idx]
                e_p = [jnp.exp2(lse_p[i] - mx[i]) for i in idx]
                e_c = [jnp.exp2(lse[i] - mx[i]) for i in idx]
                tot = [e_p[i] + e_c[i] for i in idx]
                o = [(os_[pl.ds(u0[i], tq), :] * e_p[i] + o[i] * e_c[i]) * (1.0 / tot[i]) for i in idx]
                lse = [mx[i] + jnp.log2(tot[i]) for i in idx]
            for i in idx:
                os_[pl.ds(u0[i], tq), :] = o[i]
                if not last:
                    ls_[pl.ds(u0[i], tq), :] = lse[i]
            return carry

        lax.fori_loop(0, seq // (tq * BAND_GROUP), body, 0)
    o_src, o_dst = st_src[0], st_dst[0]
    for gi in range(n_pairs - 1, 0, -1):
        _ungroup(o_src, o_dst, dils[gi - 1], dils[gi] // dils[gi - 1], seq)
        o_src, o_dst = o_dst, o_src
    o_ref[...] = o_src[...].astype(o_ref.dtype)


def dilated_attention(qkv, batch, seq):
    for w, d in DILATED_PAIRS:
        assert (w // 2) // d == BAND_HALF and seq % (d * LANES) == 0
    col = lambda off: pl.BlockSpec((None, seq, LANES), lambda b, h, off=off: (b, 0, off + h))
    return pl.pallas_call(
        functools.partial(_dilated_kernel, seq=seq),
        grid=(batch, N_HEADS_A),
        in_specs=[col(0), col(N_HEADS_A), col(2 * N_HEADS_A)],
        out_specs=pl.BlockSpec((None, seq, LANES), lambda b, h: (b, 0, h)),
        out_shape=jax.ShapeDtypeStruct((batch, seq, N_HEADS_A * HEAD_DIM), BF16),
        scratch_shapes=[pltpu.VMEM((seq, LANES), F32)] * 6 + [pltpu.VMEM((seq, LANES), BF16)] * 3
        + [pltpu.VMEM((seq, LANES), F32)] * 4,
        compiler_params=_params("parallel", "parallel"),
        name="dilated_attn",
    )(qkv, qkv, qkv)


def na_bias_table(rpb):
    cols = jnp.arange(GRID_W)
    cs = jnp.clip(cols - NA_KW // 2, 0, GRID_W - NA_KW)
    col_mask = (cols[None, :] >= cs[:, None]) & (cols[None, :] < cs[:, None] + NA_KW)
    coff = jnp.clip(cols[None, :] - cols[:, None], -(NA_KW - 1), NA_KW - 1) + (NA_KW - 1)
    roff = jnp.arange(NA_KH)[None, :] - jnp.arange(NA_KH)[:, None] + (NA_KH - 1)
    row_hot = jax.nn.one_hot(roff, 2 * NA_KH - 1, dtype=F32)
    col_hot = jax.nn.one_hot(coff, 2 * NA_KW - 1, dtype=F32)
    bias = jnp.einsum("hrc,djr,qkc->hdqjk", rpb.astype(F32), row_hot, col_hot, precision=lax.Precision.HIGHEST)
    bias = jnp.where(col_mask[None, None, :, None, :], bias * LOG2E, NEG_BIG)
    return bias.reshape(rpb.shape[0], NA_KH, GRID_W, NA_KH * GRID_W)


def _na_kernel(q_ref, k_ref, v_ref, b_ref, o_ref, *, rows):
    nk = NA_KH * GRID_W

    group = 8

    def body(g, carry):
        rr = [g * group + i for i in range(group)]
        rs = [jnp.clip(r - NA_KH // 2, 0, rows - NA_KH) for r in rr]
        k0 = [pl.multiple_of(x * GRID_W, GRID_W) for x in rs]
        q0 = [pl.multiple_of(r * GRID_W, GRID_W) for r in rr]
        s = [lax.dot_general(q_ref[pl.ds(q0[i], GRID_W), :], k_ref[pl.ds(k0[i], nk), :], (((1,), (1,)), ((), ())),
                             preferred_element_type=F32) + b_ref[rr[i] - rs[i]] for i in range(group)]
        m = [jnp.max(x, axis=1, keepdims=True) for x in s]
        p = [jnp.exp2(s[i] - m[i]) for i in range(group)]
        l = [jnp.sum(x, axis=1, keepdims=True) for x in p]
        o = [jnp.dot(p[i].astype(BF16), v_ref[pl.ds(k0[i], nk), :], preferred_element_type=F32) * (1.0 / l[i])
             for i in range(group)]
        for i in range(group):
            o_ref[pl.ds(q0[i], GRID_W), :] = o[i].astype(o_ref.dtype)
        return carry

    lax.fori_loop(0, rows // group, body, 0)


def neighbourhood_attention(qkv, bias, layer, batch, seq):
    rows = seq // GRID_W
    assert rows >= NA_KH and seq % GRID_W == 0
    base = 3 * N_HEADS_A
    col = lambda off: pl.BlockSpec((None, seq, LANES), lambda b, h, off=off: (b, 0, base + off + h))
    return pl.pallas_call(
        functools.partial(_na_kernel, rows=rows),
        grid=(batch, N_HEADS_B),
        in_specs=[col(0), col(N_HEADS_B), col(2 * N_HEADS_B),
                  pl.BlockSpec((None, None, NA_KH, GRID_W, NA_KH * GRID_W), lambda b, h: (layer, h, 0, 0, 0))],
        out_specs=pl.BlockSpec((None, seq, LANES), lambda b, h: (b, 0, h)),
        out_shape=jax.ShapeDtypeStruct((batch, seq, N_HEADS_B * HEAD_DIM), BF16),
        compiler_params=_params("parallel", "parallel"),
        name="na_attn",
    )(qkv, qkv, qkv, bias)


def _diff_kernel(q_ref, k_ref, v_ref, lq1_ref, lk1_ref, lq2_ref, lk2_ref, g_ref, o_ref, *, lam_init, tk):
    lam = (jnp.exp(jnp.sum(lq1_ref[...] * lk1_ref[...], axis=1, keepdims=True))
           - jnp.exp(jnp.sum(lq2_ref[...] * lk2_ref[...], axis=1, keepdims=True)) + lam_init)
    seq = k_ref.shape[0]
    state = [None, None]
    for j in range(seq // tk):
        v = v_ref[j * tk:(j + 1) * tk, :]
        for c in range(2):
            q = q_ref[:, c * LANES:(c + 1) * LANES]
            k = k_ref[j * tk:(j + 1) * tk, c * LANES:(c + 1) * LANES]
            s = lax.dot_general(q, k, (((1,), (1,)), ((), ())), preferred_element_type=F32)
            m_c = jnp.max(s, axis=1, keepdims=True)
            if state[c] is None:
                p = jnp.exp2(s - m_c)
                state[c] = (m_c, jnp.sum(p, axis=1, keepdims=True),
                            jnp.dot(p.astype(BF16), v, preferred_element_type=F32))
            else:
                m, l, acc = state[c]
                m_new = jnp.maximum(m, m_c)
                alpha = jnp.exp2(m - m_new)
                p = jnp.exp2(s - m_new)
                state[c] = (m_new, alpha * l + jnp.sum(p, axis=1, keepdims=True),
                            alpha * acc + jnp.dot(p.astype(BF16), v, preferred_element_type=F32))
    o = state[0][2] * (1.0 / state[0][1]) - lam * (state[1][2] * (1.0 / state[1][1]))
    o = o * _rms_scale(o) * g_ref[...] * (1.0 - lam_init)
    o_ref[...] = o.astype(o_ref.dtype)


def diff_attention(qkv, lq1, lk1, lq2, lk2, subln, layer, layer_idx, batch, seq, *, tq=256, tk=1024):
    hw = 2 * HEAD_DIM_C
    lam_init = 0.8 - 0.6 * math.exp(-0.3 * layer_idx)
    vec = lambda n: pl.BlockSpec((None, 1, n), lambda b, h, t: (layer, 0, 0))
    return pl.pallas_call(
        functools.partial(_diff_kernel, lam_init=lam_init, tk=min(tk, seq)),
        grid=(batch, N_HEADS_C, seq // tq),
        in_specs=[pl.BlockSpec((None, tq, hw), lambda b, h, t: (b, t, h)),
                  pl.BlockSpec((None, seq, hw), lambda b, h, t: (b, 0, N_HEADS_C + h)),
                  pl.BlockSpec((None, seq, hw), lambda b, h, t: (b, 0, 2 * N_HEADS_C + h)),
                  vec(HEAD_DIM_C), vec(HEAD_DIM_C), vec(HEAD_DIM_C), vec(HEAD_DIM_C), vec(hw)],
        out_specs=pl.BlockSpec((None, tq, hw), lambda b, h, t: (b, t, h)),
        out_shape=jax.ShapeDtypeStruct((batch, seq, N_HEADS_C * hw), BF16),
        compiler_params=_params("parallel", "parallel", "arbitrary"),
        name="diff_attn",
    )(qkv, qkv, qkv, lq1, lk1, lq2, lk2, subln)


def _proj_res_kernel(*refs, n_in):
    a_refs, w_ref, x_ref, o_ref = refs[:n_in], refs[n_in], refs[n_in + 1], refs[n_in + 2]
    acc = x_ref[...]
    k0 = 0
    for a_ref in a_refs:
        kk = a_ref.shape[1]
        acc = acc + jnp.dot(a_ref[...], w_ref[k0:k0 + kk, :], preferred_element_type=F32)
        k0 += kk
    o_ref[...] = acc


def proj_residual(acts, w, layer, x2d, *, tm=1024, tn=1024):
    n, d = x2d.shape
    ktot = w.shape[1]
    assert sum(a.shape[1] for a in acts) == ktot and n % tm == 0 and d % tn == 0
    return pl.pallas_call(
        functools.partial(_proj_res_kernel, n_in=len(acts)),
        grid=(n // tm, d // tn),
        in_specs=[pl.BlockSpec((tm, a.shape[1]), lambda i, j: (i, 0)) for a in acts]
        + [pl.BlockSpec((None, ktot, tn), lambda i, j: (layer, 0, j)), pl.BlockSpec((tm, tn), lambda i, j: (i, j))],
        out_specs=pl.BlockSpec((tm, tn), lambda i, j: (i, j)),
        out_shape=jax.ShapeDtypeStruct((n, d), F32),
        compiler_params=_params("parallel", "parallel"),
        name="proj_residual",
    )(*acts, w, x2d)


def _swiglu_kernel(x_ref, g_ref, wg_ref, wu_ref, wd_ref, o_ref, h_ref, acc_ref):
    f = pl.program_id(1)

    @pl.when(f == 0)
    def _():
        x = x_ref[...]
        h_ref[...] = (x * _rms_scale(x) * g_ref[...]).astype(BF16)
        acc_ref[...] = x

    h = h_ref[...]
    gt = jnp.dot(h, wg_ref[...], preferred_element_type=F32)
    up = jnp.dot(h, wu_ref[...], preferred_element_type=F32)
    a = gt * jax.nn.sigmoid(gt) * up
    acc_ref[...] += jnp.dot(a.astype(BF16), wd_ref[...], preferred_element_type=F32)

    @pl.when(f == pl.num_programs(1) - 1)
    def _():
        o_ref[...] = acc_ref[...]


def swiglu_residual(x2d, gain, wg, wu, wd, layer, *, tm=512, tf=512):
    n, d = x2d.shape
    ff = wg.shape[2]
    assert n % tm == 0 and ff % tf == 0
    return pl.pallas_call(
        _swiglu_kernel,
        grid=(n // tm, ff // tf),
        in_specs=[pl.BlockSpec((tm, d), lambda i, f: (i, 0)),
                  pl.BlockSpec((None, 1, d), lambda i, f: (layer, 0, 0)),
                  pl.BlockSpec((None, d, tf), lambda i, f: (layer, 0, f)),
                  pl.BlockSpec((None, d, tf), lambda i, f: (layer, 0, f)),
                  pl.BlockSpec((None, tf, d), lambda i, f: (layer, f, 0))],
        out_specs=pl.BlockSpec((tm, d), lambda i, f: (i, 0)),
        out_shape=jax.ShapeDtypeStruct((n, d), F32),
        scratch_shapes=[pltpu.VMEM((tm, d), BF16), pltpu.VMEM((tm, d), F32)],
        compiler_params=_params("parallel", "arbitrary"),
        name="swiglu",
    )(x2d, gain, wg, wu, wd)


SEL_E1, SEL_E2, SEL_R1, SEL_R2 = 0, 1, 2, 3


def _router_kernel(x_ref, g_ref, r_ref, sel_ref, gate_ref, cnt_ref, tri_ref):
    tm = x_ref.shape[0]

    @pl.when(pl.program_id(0) == 0)
    def _():
        cnt_ref[...] = jnp.zeros_like(cnt_ref)
        row = lax.broadcasted_iota(I32, (tm, tm), 0)
        col = lax.broadcasted_iota(I32, (tm, tm), 1)
        tri_ref[...] = jnp.where(col < row, 1.0, 0.0).astype(BF16)

    x = x_ref[...]
    h = x * _rms_scale(x) * g_ref[...]
    logits = jnp.dot(h, r_ref[...], preferred_element_type=F32, precision=lax.Precision.HIGHEST)
    lane = lax.broadcasted_iota(I32, logits.shape, 1)
    logits = jnp.where(lane < N_EXPERTS, logits, -jnp.inf)
    m1 = jnp.max(logits, axis=1, keepdims=True)
    i1 = jnp.min(jnp.where(logits == m1, lane, LANES), axis=1, keepdims=True)
    rest = jnp.where(lane == i1, -jnp.inf, logits)
    m2 = jnp.max(rest, axis=1, keepdims=True)
    i2 = jnp.min(jnp.where(rest == m2, lane, LANES), axis=1, keepdims=True)
    e2 = jnp.exp(m2 - m1)
    g1 = 1.0 / (1.0 + e2)
    g2 = e2 / (1.0 + e2)
    hot = jnp.where((lane == i1) | (lane == i2), 1.0, 0.0)
    rank = jnp.dot(tri_ref[...], hot.astype(BF16), preferred_element_type=F32) + cnt_ref[...]
    r1 = jnp.sum(jnp.where(lane == i1, rank, 0.0), axis=1, keepdims=True).astype(I32)
    r2 = jnp.sum(jnp.where(lane == i2, rank, 0.0), axis=1, keepdims=True).astype(I32)
    sel = jnp.where(lane == SEL_E1, i1, 0) + jnp.where(lane == SEL_E2, i2, 0)
    sel = sel + jnp.where(lane == SEL_R1, r1, 0) + jnp.where(lane == SEL_R2, r2, 0)
    sel_ref[...] = sel
    gate_ref[...] = jnp.where(lane == 0, g1, 0.0) + jnp.where(lane == 1, g2, 0.0)
    cnt_ref[...] += jnp.sum(hot, axis=0, keepdims=True)


def router(x2d, gain, r_pad, layer, *, tm=512):
    n, d = x2d.shape
    return pl.pallas_call(
        _router_kernel,
        grid=(n // tm,),
        in_specs=[pl.BlockSpec((tm, d), lambda i: (i, 0)), pl.BlockSpec((None, 1, d), lambda i: (layer, 0, 0)),
                  pl.BlockSpec((None, d, LANES), lambda i: (layer, 0, 0))],
        out_specs=[pl.BlockSpec((tm, LANES), lambda i: (i, 0)), pl.BlockSpec((tm, LANES), lambda i: (i, 0)),
                   pl.BlockSpec((1, LANES), lambda i: (0, 0))],
        out_shape=[jax.ShapeDtypeStruct((n, LANES), I32), jax.ShapeDtypeStruct((n, LANES), F32),
                   jax.ShapeDtypeStruct((1, LANES), F32)],
        scratch_shapes=[pltpu.VMEM((tm, tm), BF16)],
        compiler_params=_params("arbitrary"),
        name="router",
    )(x2d, gain, r_pad)


def _row_copies(src_of, dst_of, sem, rows, wait):
    def body(r, carry):
        cp = pltpu.make_async_copy(src_of(r), dst_of(r), sem)
        if wait:
            cp.wait()
        else:
            cp.start()
        return carry
    lax.fori_loop(0, rows, body, 0, unroll=8)


def _dispatch_kernel(p1_ref, p2_ref, x_ref, g_ref, hs_ref, hbuf, sem):
    tm = x_ref.shape[0]
    x = x_ref[...]
    hbuf[...] = x * _rms_scale(x) * g_ref[...]
    src = lambda r: hbuf.at[pl.ds(r, 1)]
    for p_ref in (p1_ref, p2_ref):
        _row_copies(src, lambda r, p_ref=p_ref: hs_ref.at[pl.ds(p_ref[0, r], 1)], sem, tm, wait=False)
    for p_ref in (p1_ref, p2_ref):
        _row_copies(src, lambda r, p_ref=p_ref: hs_ref.at[pl.ds(p_ref[0, r], 1)], sem, tm, wait=True)


def dispatch(x2d, gain, layer, pos1, pos2, *, tm=256):
    n, d = x2d.shape
    pspec = pl.BlockSpec((None, 1, tm), lambda i: (i, 0, 0), memory_space=pltpu.SMEM)
    return pl.pallas_call(
        _dispatch_kernel,
        grid=(n // tm,),
        in_specs=[pspec, pspec, pl.BlockSpec((tm, d), lambda i: (i, 0)),
                  pl.BlockSpec((None, 1, d), lambda i: (layer, 0, 0))],
        out_specs=pl.BlockSpec(memory_space=pl.ANY),
        out_shape=jax.ShapeDtypeStruct((TOP_K * n, d), F32),
        scratch_shapes=[pltpu.VMEM((tm, d), F32), pltpu.SemaphoreType.DMA(())],
        compiler_params=_params("arbitrary"),
        name="moe_dispatch",
    )(pos1.reshape(n // tm, 1, tm), pos2.reshape(n // tm, 1, tm), x2d, gain)


VISIT_VALID, VISIT_FIRST = 1, 2


def _grouped_swiglu_kernel(tile_ref, exp_ref, flag_ref, offs_ref, hs_ref, wg_ref, wu_ref, wd_ref, o_ref, h_ref, acc_ref):
    v, f = pl.program_id(0), pl.program_id(1)
    tm = hs_ref.shape[0]
    flags = flag_ref[v]

    @pl.when((flags & VISIT_VALID) != 0)
    def _():
        @pl.when(f == 0)
        def _():
            h_ref[...] = hs_ref[...].astype(BF16)
            acc_ref[...] = jnp.zeros_like(acc_ref)

        h = h_ref[...]
        gt = jnp.dot(h, wg_ref[...], preferred_element_type=F32)
        up = jnp.dot(h, wu_ref[...], preferred_element_type=F32)
        a = gt * jax.nn.sigmoid(gt) * up
        acc_ref[...] += jnp.dot(a.astype(BF16), wd_ref[...], preferred_element_type=F32)

        @pl.when(f == pl.num_programs(1) - 1)
        def _():
            e = exp_ref[v]
            row = tile_ref[v] * tm + lax.broadcasted_iota(I32, (tm, 1), 0)
            mine = (row >= offs_ref[e]) & (row < offs_ref[e + 1])

            @pl.when((flags & VISIT_FIRST) != 0)
            def _():
                o_ref[...] = jnp.where(mine, acc_ref[...], 0.0)

            @pl.when((flags & VISIT_FIRST) == 0)
            def _():
                o_ref[...] = jnp.where(mine, acc_ref[...], o_ref[...])


def grouped_swiglu(hs, wg, wu, wd, layer, tile_ids, exp_ids, flags, offs, *, tm=512, tf=512):
    ns, d = hs.shape
    ff = wg.shape[3]
    nv = tile_ids.shape[0]
    grid_spec = pltpu.PrefetchScalarGridSpec(
        num_scalar_prefetch=4,
        grid=(nv, ff // tf),
        in_specs=[pl.BlockSpec((tm, d), lambda v, f, t, e, fl, o: (t[v], 0)),
                  pl.BlockSpec((None, None, d, tf), lambda v, f, t, e, fl, o: (layer, e[v], 0, f)),
                  pl.BlockSpec((None, None, d, tf), lambda v, f, t, e, fl, o: (layer, e[v], 0, f)),
                  pl.BlockSpec((None, None, tf, d), lambda v, f, t, e, fl, o: (layer, e[v], f, 0))],
        out_specs=pl.BlockSpec((tm, d), lambda v, f, t, e, fl, o: (t[v], 0)),
        scratch_shapes=[pltpu.VMEM((tm, d), BF16), pltpu.VMEM((tm, d), F32)],
    )
    return pl.pallas_call(
        _grouped_swiglu_kernel,
        grid_spec=grid_spec,
        out_shape=jax.ShapeDtypeStruct((ns, d), F32),
        compiler_params=_params("arbitrary", "arbitrary"),
        name="moe_grouped_swiglu",
    )(tile_ids, exp_ids, flags, offs, hs, wg, wu, wd)


def _combine_kernel(p1_ref, p2_ref, x_ref, gate_ref, ys_ref, o_ref, buf1, buf2, sem):
    tm = x_ref.shape[0]
    for p_ref, buf in ((p1_ref, buf1), (p2_ref, buf2)):
        _row_copies(lambda r, p_ref=p_ref: ys_ref.at[pl.ds(p_ref[0, r], 1)], lambda r, buf=buf: buf.at[pl.ds(r, 1)],
                    sem, tm, wait=False)
    for p_ref, buf in ((p1_ref, buf1), (p2_ref, buf2)):
        _row_copies(lambda r, p_ref=p_ref: ys_ref.at[pl.ds(p_ref[0, r], 1)], lambda r, buf=buf: buf.at[pl.ds(r, 1)],
                    sem, tm, wait=True)
    gate = gate_ref[...]
    o_ref[...] = x_ref[...] + gate[:, 0:1] * buf1[...] + gate[:, 1:2] * buf2[...]


def combine(x2d, gates, ys, pos1, pos2, *, tm=256):
    n, d = x2d.shape
    pspec = pl.BlockSpec((None, 1, tm), lambda i: (i, 0, 0), memory_space=pltpu.SMEM)
    return pl.pallas_call(
        _combine_kernel,
        grid=(n // tm,),
        in_specs=[pspec, pspec, pl.BlockSpec((tm, d), lambda i: (i, 0)), pl.BlockSpec((tm, LANES), lambda i: (i, 0)),
                  pl.BlockSpec(memory_space=pl.ANY)],
        out_specs=pl.BlockSpec((tm, d), lambda i: (i, 0)),
        out_shape=jax.ShapeDtypeStruct((n, d), F32),
        scratch_shapes=[pltpu.VMEM((tm, d), F32), pltpu.VMEM((tm, d), F32), pltpu.SemaphoreType.DMA(())],
        compiler_params=_params("arbitrary"),
        name="moe_combine",
    )(pos1.reshape(n // tm, 1, tm), pos2.reshape(n // tm, 1, tm), x2d, gates, ys)


def routing_tables(sel, counts, n_slots, tm):
    cnt = counts[0, :N_EXPERTS].astype(I32)
    offs = jnp.concatenate([jnp.zeros((1,), I32), jnp.cumsum(cnt)])
    experts = jnp.arange(N_EXPERTS, dtype=I32)
    off_of = lambda e: jnp.sum(jnp.where(e[:, None] == experts[None, :], offs[None, :N_EXPERTS], 0), axis=1)
    pos1 = off_of(sel[:, SEL_E1]) + sel[:, SEL_R1]
    pos2 = off_of(sel[:, SEL_E2]) + sel[:, SEL_R2]
    n_tiles = n_slots // tm
    n_visits = n_tiles + N_EXPERTS - 1
    first_tile = offs[:-1] // tm
    tiles_of = jnp.where(cnt > 0, (offs[1:] - 1) // tm - first_tile + 1, 0)
    v_end = jnp.cumsum(tiles_of)
    v_start = v_end - tiles_of
    v = jnp.arange(n_visits, dtype=I32)
    valid = v < v_end[-1]
    e_of = jnp.minimum(jnp.sum((v[:, None] >= v_end[None, :]).astype(I32), axis=1), N_EXPERTS - 1)
    pick = lambda table: jnp.sum(jnp.where(e_of[:, None] == experts[None, :], table[None, :], 0), axis=1)
    tile = pick(first_tile) + v - pick(v_start)
    last_e = jnp.max(jnp.where(cnt > 0, experts, 0))
    tile = jnp.where(valid, tile, n_tiles - 1).astype(I32)
    e_of = jnp.where(valid, e_of, last_e).astype(I32)
    prev_tile = jnp.concatenate([jnp.full((1,), -1, I32), tile[:-1]])
    flags = jnp.where(valid, VISIT_VALID, 0) + jnp.where(tile != prev_tile, VISIT_FIRST, 0)
    return pos1, pos2, tile, e_of, flags.astype(I32), offs


def moe_residual(x2d, p, layer, *, tm=512):
    n, d = x2d.shape
    sel, gates, counts = router(x2d, p["od_norm_ffn"], p["od_router"], layer)
    pos1, pos2, tile, e_of, flags, offs = routing_tables(sel, counts, TOP_K * n, tm)
    hs = dispatch(x2d, p["od_norm_ffn"], layer, pos1, pos2)
    ys = grouped_swiglu(hs, p["od_w_gate"], p["od_w_up"], p["od_w_down"], layer, tile, e_of, flags, offs, tm=tm)
    return combine(x2d, gates, ys, pos1, pos2)


def _final_norm_kernel(x_ref, g_ref, o_ref):
    x = x_ref[...]
    o_ref[...] = x * _rms_scale(x) * g_ref[...]


def final_norm(x2d, gain, *, tm=512):
    n, d = x2d.shape
    return pl.pallas_call(
        _final_norm_kernel,
        grid=(n // tm,),
        in_specs=[pl.BlockSpec((tm, d), lambda i: (i, 0)), pl.BlockSpec((1, d), lambda i: (0, 0))],
        out_specs=pl.BlockSpec((tm, d), lambda i: (i, 0)),
        out_shape=jax.ShapeDtypeStruct((n, d), F32),
        compiler_params=_params("parallel"),
        name="final_norm",
    )(x2d, gain.reshape(1, d))


def rope_tables(seq):
    half = HEAD_DIM // 2
    inv = jnp.power(jnp.float32(ROPE_THETA), -jnp.arange(half, dtype=F32) / half)
    ang = jnp.arange(seq).astype(F32)[:, None] * inv[None, :]
    cos, sin = jnp.cos(ang), jnp.sin(ang)
    return jnp.concatenate([cos, cos], axis=1), jnp.concatenate([-sin, sin], axis=1)


EVEN_MODES = (2, 1, 0, 3, 0, 0)
ODD_MODES = (2, 2, 1, 1, 0, 0)


def trunk(x, p):
    batch, seq, d = x.shape
    n = batch * seq
    cos, sin = rope_tables(seq)
    x2 = x.reshape(n, d)
    for i in range(DEPTH):
        j = i // 2
        if i % 2 == 0:
            qkv = norm_proj(x2, p["ev_norm_mix"], p["ev_w_in"], j, cos, sin, EVEN_MODES, seq)
            qkv3 = qkv.reshape(batch, seq, W_IN)
            oa = dilated_attention(qkv3, batch, seq)
            ob = neighbourhood_attention(qkv3, p["ev_bias"], j, batch, seq)
            x2 = proj_residual([oa.reshape(n, -1), ob.reshape(n, -1)], p["ev_w_out"], j, x2)
            x2 = swiglu_residual(x2, p["ev_norm_ffn"], p["ev_w_gate"], p["ev_w_up"], p["ev_w_down"], j)
        else:
            qkv = norm_proj(x2, p["od_norm_mix"], p["od_w_qkv"], j, cos, sin, ODD_MODES, seq)
            o = diff_attention(qkv.reshape(batch, seq, W_IN), p["od_lambda_q1"], p["od_lambda_k1"],
                               p["od_lambda_q2"], p["od_lambda_k2"], p["od_subln"], j, i, batch, seq)
            x2 = proj_residual([o.reshape(n, -1)], p["od_w_out"], j, x2)
            x2 = moe_residual(x2, p, j)
    return final_norm(x2, p["final_norm"]).reshape(batch, seq, d)


def kernel(x_prompt, x_sample, ev_norm_mix, ev_w_in, ev_w_out, ev_rpb, ev_norm_ffn, ev_w_gate, ev_w_up, ev_w_down, od_norm_mix, od_w_qkv, od_w_out, od_lambda_q1, od_lambda_k1, od_lambda_q2, od_lambda_k2, od_subln, od_norm_ffn, od_router, od_w_gate, od_w_up, od_w_down, final_norm):
    bf = lambda w: w.astype(BF16)
    r_pad = jnp.pad(od_router.astype(F32), ((0, 0), (0, 0), (0, LANES - N_EXPERTS)))
    vec = lambda t: t.reshape(t.shape[0], 1, t.shape[1])
    p = dict(
        ev_norm_mix=vec(ev_norm_mix), ev_w_in=bf(ev_w_in), ev_w_out=bf(ev_w_out),
        ev_bias=jax.vmap(na_bias_table)(ev_rpb), ev_norm_ffn=vec(ev_norm_ffn),
        ev_w_gate=bf(ev_w_gate), ev_w_up=bf(ev_w_up), ev_w_down=bf(ev_w_down),
        od_norm_mix=vec(od_norm_mix), od_w_qkv=bf(od_w_qkv), od_w_out=bf(od_w_out),
        od_lambda_q1=vec(od_lambda_q1), od_lambda_k1=vec(od_lambda_k1), od_lambda_q2=vec(od_lambda_q2),
        od_lambda_k2=vec(od_lambda_k2), od_subln=vec(od_subln), od_norm_ffn=vec(od_norm_ffn), od_router=r_pad,
        od_w_gate=bf(od_w_gate), od_w_up=bf(od_w_up), od_w_down=bf(od_w_down), final_norm=final_norm,
    )
    return (trunk(x_prompt, p), trunk(x_sample, p))
```
